```python
import math
import jax, jax.numpy as jnp
from jax import lax
import numpy as np

D_MODEL = 1024
BATCH = 2
SEQ = 8192
DEPTH = 1
DEC_BATCH = 4
DEC_SEQ = 8192
PAST_LEN = 128

PLE_DIM = 256
GRID_W = 64
Q_BLOCK = 128
EPS = 1e-6
A_HEADS = 8
A_HEAD_DIM = 64
A_ROT_DIM = A_HEAD_DIM // 4
A_ROPE_THETA = 500000.0
B_HEADS = 16
B_KV_HEADS = 4
B_HEAD_DIM = 64
B_AXIAL_THETA = 10000.0
D_FF = 2816
CONV_W = 3

A_Q = A_HEADS * 2 * A_HEAD_DIM
A_K = A_HEADS * 2 * A_HEAD_DIM
A_V = A_HEADS * 2 * A_HEAD_DIM
B_Q = B_HEADS * B_HEAD_DIM
B_KV = B_KV_HEADS * B_HEAD_DIM
A_WIDTH = A_V
B_WIDTH = B_Q
IN_COLS = A_Q + A_K + A_V + B_Q + 2 * B_KV + 2 * D_MODEL
IN_OFFSETS = (A_Q, A_Q + A_K, A_Q + A_K + A_V, A_Q + A_K + A_V + B_Q,
              A_Q + A_K + A_V + B_Q + B_KV, A_Q + A_K + A_V + B_Q + 2 * B_KV,
              A_Q + A_K + A_V + B_Q + 2 * B_KV + D_MODEL)

kernel_name = "hybrid_diffattn_axial_gqa_encoder"


def rms_norm(x, g):
    x32 = x.astype(jnp.float32)
    y = x32 * lax.rsqrt(jnp.mean(x32 * x32, axis=-1, keepdims=True) + EPS)
    return (y * g.astype(jnp.float32)).astype(x.dtype)


def rope_tables(pos, dim, theta):
    inv = theta ** (-jnp.arange(0, dim, 2, dtype=jnp.float32) / dim)
    ang = pos.astype(jnp.float32)[:, None] * inv[None, :]
    return jnp.cos(ang), jnp.sin(ang)


def rotate(x, cos, sin):
    half = x.shape[-1] // 2
    x1, x2 = x[..., :half], x[..., half:]
    c = cos[None, :, None, :].astype(x.dtype)
    s = sin[None, :, None, :].astype(x.dtype)
    return jnp.concatenate([x1 * c - x2 * s, x2 * c + x1 * s], axis=-1)


def partial_rope(x, cos, sin):
    rd = 2 * cos.shape[-1]
    return jnp.concatenate([rotate(x[..., :rd], cos, sin), x[..., rd:]], axis=-1)


def axial_rope(x, cos_r, sin_r, cos_c, sin_c):
    half = x.shape[-1] // 2
    return jnp.concatenate([rotate(x[..., :half], cos_r, sin_r),
                            rotate(x[..., half:], cos_c, sin_c)], axis=-1)


def to_blocks(x):
    b, s = x.shape[:2]
    return jnp.moveaxis(x.reshape((b, s // Q_BLOCK, Q_BLOCK) + x.shape[2:]), 1, 0)


def from_blocks(y):
    nb, b, qb = y.shape[:3]
    return jnp.moveaxis(y, 0, 1).reshape((b, nb * qb) + y.shape[3:])


def diff_attention(q1, q2, k1, k2, v, lam):
    scale = A_HEAD_DIM ** -0.5

    def block(qb):
        q1b, q2b = qb
        s1 = jnp.einsum('bqhd,bkhd->bhqk', q1b, k1).astype(jnp.float32) * scale
        s2 = jnp.einsum('bqhd,bkhd->bhqk', q2b, k2).astype(jnp.float32) * scale
        a = jax.nn.softmax(s1, axis=-1) - lam * jax.nn.softmax(s2, axis=-1)
        return jnp.einsum('bhqk,bkhe->bqhe', a.astype(v.dtype), v)

    return from_blocks(lax.map(block, (to_blocks(q1), to_blocks(q2))))


def gqa_attention(q, k, v):
    b, s = q.shape[:2]
    rep = B_HEADS // B_KV_HEADS
    scale = B_HEAD_DIM ** -0.5
    qg = q.reshape(b, s, B_KV_HEADS, rep, B_HEAD_DIM)

    def block(qb):
        sc = jnp.einsum('bqgrd,bkgd->bgrqk', qb, k).astype(jnp.float32) * scale
        pr = jax.nn.softmax(sc, axis=-1)
        return jnp.einsum('bgrqk,bkgd->bqgrd', pr.astype(v.dtype), v)

    out = from_blocks(lax.map(block, to_blocks(qg)))
    return out.reshape(b, s, B_WIDTH)


def depthwise_conv_centred(u, w, bias):
    s = u.shape[1]
    up = jnp.pad(u, ((0, 0), (1, 1), (0, 0)))
    return up[:, :s] * w[0] + up[:, 1:s + 1] * w[1] + up[:, 2:s + 2] * w[2] + bias


def encode(x, p, g_mix, w_in, lambda_q1, lambda_k1, lambda_q2, lambda_k2, g_diff, w_a,
           g_qn, g_kn, w_b, w_out, g_ffn, w_up, conv_w, conv_b, w_down,
           w_ple, g_ple, w_ple_gate, g_final):
    b, s, _ = x.shape
    rows = s // GRID_W
    pos = jnp.arange(s, dtype=jnp.int32)
    row_ids = jnp.repeat(jnp.arange(rows, dtype=jnp.int32), GRID_W)
    col_ids = jnp.tile(jnp.arange(GRID_W, dtype=jnp.int32), rows)
    cos_a, sin_a = rope_tables(pos, A_ROT_DIM, A_ROPE_THETA)
    cos_r, sin_r = rope_tables(row_ids, B_HEAD_DIM // 2, B_AXIAL_THETA)
    cos_c, sin_c = rope_tables(col_ids, B_HEAD_DIM // 2, B_AXIAL_THETA)

    h = x
    for i in range(DEPTH):
        lam_init = 0.8 - 0.6 * math.exp(-0.3 * i)
        n = rms_norm(h, g_mix[i])
        proj = n @ w_in[i]
        aq, ak, av, bq, bk, bv, ga, gb = jnp.split(proj, IN_OFFSETS, axis=-1)

        aq = aq.reshape(b, s, A_HEADS, 2, A_HEAD_DIM)
        ak = ak.reshape(b, s, A_HEADS, 2, A_HEAD_DIM)
        av = av.reshape(b, s, A_HEADS, 2 * A_HEAD_DIM)
        q1 = partial_rope(aq[..., 0, :], cos_a, sin_a)
        q2 = partial_rope(aq[..., 1, :], cos_a, sin_a)
        k1 = partial_rope(ak[..., 0, :], cos_a, sin_a)
        k2 = partial_rope(ak[..., 1, :], cos_a, sin_a)
        lam = (jnp.exp(jnp.sum(lambda_q1[i].astype(jnp.float32) * lambda_k1[i].astype(jnp.float32)))
               - jnp.exp(jnp.sum(lambda_q2[i].astype(jnp.float32) * lambda_k2[i].astype(jnp.float32)))
               + lam_init)
        oa = diff_attention(q1, q2, k1, k2, av, lam)
        oa = rms_norm(oa, g_diff[i]) * (1.0 - lam_init)
        ya = oa.reshape(b, s, A_WIDTH) @ w_a[i]

        bq = axial_rope(rms_norm(bq.reshape(b, s, B_HEADS, B_HEAD_DIM), g_qn[i]),
                        cos_r, sin_r, cos_c, sin_c)
        bk = axial_rope(rms_norm(bk.reshape(b, s, B_KV_HEADS, B_HEAD_DIM), g_kn[i]),
                        cos_r, sin_r, cos_c, sin_c)
        bv = bv.reshape(b, s, B_KV_HEADS, B_HEAD_DIM)
        yb = gqa_attention(bq, bk, bv) @ w_b[i]

        merged = jax.nn.sigmoid(ga) * ya + jax.nn.sigmoid(gb) * yb
        h = h + merged @ w_out[i]

        n2 = rms_norm(h, g_ffn[i])
        u = depthwise_conv_centred(n2 @ w_up[i], conv_w[i], conv_b[i])
        ug, uv = jnp.split(u, 2, axis=-1)
        h = h + (jax.nn.gelu(ug) * uv) @ w_down[i]

        gate = jax.nn.sigmoid(rms_norm(h, g_ple[i]) @ w_ple_gate[i])
        h = h + (p[i] @ w_ple[i]) * gate
    return rms_norm(h, g_final)


def setup_inputs(seed: int = 0) -> dict:
    key = jax.random.key(seed)
    ks = jax.random.split(key, 32)
    f32 = jnp.float32

    def dense(k, shape, fan_in):
        return jax.random.normal(k, shape, f32) * (fan_in ** -0.5)

    def gain(k, shape):
        return 1.0 + 0.01 * jax.random.normal(k, shape, f32)

    return {
        "x_prompt": jax.random.normal(ks[0], (BATCH, SEQ, D_MODEL), f32),
        "x_sample": jax.random.normal(ks[1], (DEC_BATCH, DEC_SEQ, D_MODEL), f32),
        "p_prompt": jax.random.normal(ks[2], (DEPTH, BATCH, SEQ, PLE_DIM), f32),
        "p_sample": jax.random.normal(ks[3], (DEPTH, DEC_BATCH, DEC_SEQ, PLE_DIM), f32),
        "g_mix": gain(ks[4], (DEPTH, D_MODEL)),
        "w_in": dense(ks[5], (DEPTH, D_MODEL, IN_COLS), D_MODEL),
        "lambda_q1": 0.1 * jax.random.normal(ks[6], (DEPTH, A_HEAD_DIM), f32),
        "lambda_k1": 0.1 * jax.random.normal(ks[7], (DEPTH, A_HEAD_DIM), f32),
        "lambda_q2": 0.1 * jax.random.normal(ks[8], (DEPTH, A_HEAD_DIM), f32),
        "lambda_k2": 0.1 * jax.random.normal(ks[9], (DEPTH, A_HEAD_DIM), f32),
        "g_diff": gain(ks[10], (DEPTH, 2 * A_HEAD_DIM)),
        "w_a": dense(ks[11], (DEPTH, A_WIDTH, D_MODEL), A_WIDTH),
        "g_qn": gain(ks[12], (DEPTH, B_HEAD_DIM)),
        "g_kn": gain(ks[13], (DEPTH, B_HEAD_DIM)),
        "w_b": dense(ks[14], (DEPTH, B_WIDTH, D_MODEL), B_WIDTH),
        "w_out": dense(ks[15], (DEPTH, D_MODEL, D_MODEL), D_MODEL),
        "g_ffn": gain(ks[16], (DEPTH, D_MODEL)),
        "w_up": dense(ks[17], (DEPTH, D_MODEL, 2 * D_FF), D_MODEL),
        "conv_w": dense(ks[18], (DEPTH, CONV_W, 2 * D_FF), CONV_W),
        "conv_b": 0.01 * jax.random.normal(ks[19], (DEPTH, 2 * D_FF), f32),
        "w_down": dense(ks[20], (DEPTH, D_FF, D_MODEL), D_FF),
        "w_ple": dense(ks[21], (DEPTH, PLE_DIM, D_MODEL), PLE_DIM),
        "g_ple": gain(ks[22], (DEPTH, D_MODEL)),
        "w_ple_gate": dense(ks[23], (DEPTH, D_MODEL, D_MODEL), D_MODEL),
        "g_final": gain(ks[24], (D_MODEL,)),
    }


def reference(x_prompt, x_sample, p_prompt, p_sample, g_mix, w_in, lambda_q1, lambda_k1,
              lambda_q2, lambda_k2, g_diff, w_a, g_qn, g_kn, w_b, w_out, g_ffn, w_up,
              conv_w, conv_b, w_down, w_ple, g_ple, w_ple_gate, g_final):
    y_prompt = encode(x_prompt, p_prompt, g_mix, w_in, lambda_q1, lambda_k1, lambda_q2, lambda_k2,
                      g_diff, w_a, g_qn, g_kn, w_b, w_out, g_ffn, w_up, conv_w, conv_b, w_down,
                      w_ple, g_ple, w_ple_gate, g_final)
    y_sample = encode(x_sample, p_sample, g_mix, w_in, lambda_q1, lambda_k1, lambda_q2, lambda_k2,
                      g_diff, w_a, g_qn, g_kn, w_b, w_out, g_ffn, w_up, conv_w, conv_b, w_down,
                      w_ple, g_ple, w_ple_gate, g_final)
    return (y_prompt, y_sample)
```

```python
import functools
import math

import jax
import jax.numpy as jnp
from jax import lax
from jax.experimental import pallas as pl
from jax.experimental.pallas import tpu as pltpu

F32 = jnp.float32
BF16 = jnp.bfloat16

D_MODEL = 1024
PLE_DIM = 256
GRID_W = 64
EPS = 1e-6
A_HEADS = 8
A_HEAD_DIM = 64
A_ROT_DIM = A_HEAD_DIM // 4
A_ROPE_THETA = 500000.0
B_HEADS = 16
B_KV_HEADS = 4
B_HEAD_DIM = 64
B_REP = B_HEADS // B_KV_HEADS
B_AXIAL_THETA = 10000.0
D_FF = 2816
LAM_INIT = 0.8 - 0.6 * math.exp(-0.3 * 0)

A_W = A_HEADS * 2 * A_HEAD_DIM
B_Q = B_HEADS * B_HEAD_DIM
B_KV = B_KV_HEADS * B_HEAD_DIM
N_TRANSPOSED = 3 * A_W + B_Q + 2 * B_KV

V7X_VMEM_BYTES = 64 * 1024 * 1024
VMEM_LIMIT = 56 * 1024 * 1024
SUBLANES = 8
LANES = 128

TOKEN_TILE = 512
A_Q_TILE = 512
B_Q_TILE = 256
MERGE_TILE = 512
FFN_TILE = 512
FFN_CHUNK = 256
HALO = SUBLANES
NEG_BIG = -1e30


def _rms(x, g):
    ms = jnp.mean(x * x, axis=-1, keepdims=True)
    return (x * lax.rsqrt(ms + EPS)) * g


def _rms_rows(x, g):
    ms = jnp.mean(x * x, axis=0, keepdims=True)
    return (x * lax.rsqrt(ms + EPS)) * g


def _proj_kernel(x_ref, cos_ref, sin_ref, gmix_ref, wt_ref, wg_ref, gq_ref, gk_ref,
                 qat_ref, ka_ref, vat_ref, qbt_ref, kb_ref, vbt_ref, ga_ref, gb_ref):
    x = x_ref[0]
    n = _rms(x, gmix_ref[...])
    nb = n.astype(BF16)
    nt = n.T.astype(BF16)

    cos = cos_ref[...]
    sin = sin_ref[...]
    ca, sa = cos[0:8], sin[0:8]
    cr, sr = cos[8:24], sin[8:24]
    cc, sc = cos[24:40], sin[24:40]

    def rope_a(p):
        x1, x2 = p[0:8], p[8:16]
        return jnp.concatenate([x1 * ca - x2 * sa, x2 * ca + x1 * sa, p[16:64]], axis=0)

    def rope_b(p):
        a1, a2, b1, b2 = p[0:16], p[16:32], p[32:48], p[48:64]
        return jnp.concatenate([a1 * cr - a2 * sr, a2 * cr + a1 * sr,
                                b1 * cc - b2 * sc, b2 * cc + b1 * sc], axis=0)

    def proj_t(lo, hi):
        return jnp.dot(wt_ref[lo:hi, :], nt, preferred_element_type=F32)

    scale = A_HEAD_DIM ** -0.5

    p = proj_t(0, A_W)
    for g in range(A_W // 64):
        qat_ref[0, g * 64:(g + 1) * 64, :] = (rope_a(p[g * 64:(g + 1) * 64]) * scale).astype(BF16)

    p = proj_t(A_W, 2 * A_W)
    for s in range(A_W // LANES):
        blk = jnp.concatenate([rope_a(p[s * 128:s * 128 + 64]),
                               rope_a(p[s * 128 + 64:(s + 1) * 128])], axis=0)
        ka_ref[0, :, s * LANES:(s + 1) * LANES] = blk.T.astype(BF16)

    vat_ref[0, 0] = proj_t(2 * A_W, 3 * A_W).astype(BF16)

    off = 3 * A_W
    p = proj_t(off, off + B_Q)
    gq = gq_ref[...]
    for h in range(B_HEADS):
        y = rope_b(_rms_rows(p[h * 64:(h + 1) * 64], gq))
        qbt_ref[0, h * 64:(h + 1) * 64, :] = (y * scale).astype(BF16)

    off += B_Q
    p = proj_t(off, off + B_KV)
    gk = gk_ref[...]
    zeros = jnp.zeros((LANES - B_HEAD_DIM, p.shape[1]), F32)
    for g in range(B_KV_HEADS):
        y = rope_b(_rms_rows(p[g * 64:(g + 1) * 64], gk))
        kb_ref[0, g] = jnp.concatenate([y, zeros], axis=0).T.astype(BF16)

    off += B_KV
    vbt_ref[0, 0] = proj_t(off, off + B_KV).astype(BF16)

    gates = jnp.dot(nb, wg_ref[...], preferred_element_type=F32)
    ga_ref[0] = jax.nn.sigmoid(gates[:, :D_MODEL])
    gb_ref[0] = jax.nn.sigmoid(gates[:, D_MODEL:])


def _const_spec(shape):
    nd = len(shape)
    return pl.BlockSpec(shape, lambda *_: (0,) * nd, pipeline_mode=pl.Buffered(1))


def _project(x, cos_t, sin_t, g_mix, wt, wg, gq, gk):
    b, s, d = x.shape
    t = min(TOKEN_TILE, s)
    nt = s // t
    out_shape = (
        jax.ShapeDtypeStruct((b, A_W, s), BF16),
        jax.ShapeDtypeStruct((b, s, A_W), BF16),
        jax.ShapeDtypeStruct((b, nt, A_W, t), BF16),
        jax.ShapeDtypeStruct((b, B_Q, s), BF16),
        jax.ShapeDtypeStruct((b, B_KV_HEADS, s, LANES), BF16),
        jax.ShapeDtypeStruct((b, nt, B_KV, t), BF16),
        jax.ShapeDtypeStruct((b, s, D_MODEL), F32),
        jax.ShapeDtypeStruct((b, s, D_MODEL), F32),
    )
    in_specs = [
        pl.BlockSpec((1, t, d), lambda bi, i: (bi, i, 0)),
        pl.BlockSpec((40, t), lambda bi, i: (0, i)),
        pl.BlockSpec((40, t), lambda bi, i: (0, i)),
        _const_spec((1, d)),
        _const_spec(wt.shape),
        _const_spec(wg.shape),
        _const_spec((B_HEAD_DIM, 1)),
        _const_spec((B_HEAD_DIM, 1)),
    ]
    out_specs = (
        pl.BlockSpec((1, A_W, t), lambda bi, i: (bi, 0, i)),
        pl.BlockSpec((1, t, A_W), lambda bi, i: (bi, i, 0)),
        pl.BlockSpec((1, 1, A_W, t), lambda bi, i: (bi, i, 0, 0)),
        pl.BlockSpec((1, B_Q, t), lambda bi, i: (bi, 0, i)),
        pl.BlockSpec((1, B_KV_HEADS, t, LANES), lambda bi, i: (bi, 0, i, 0)),
        pl.BlockSpec((1, 1, B_KV, t), lambda bi, i: (bi, i, 0, 0)),
        pl.BlockSpec((1, t, D_MODEL), lambda bi, i: (bi, i, 0)),
        pl.BlockSpec((1, t, D_MODEL), lambda bi, i: (bi, i, 0)),
    )
    return pl.pallas_call(
        _proj_kernel,
        grid=(b, nt),
        in_specs=in_specs,
        out_specs=out_specs,
        out_shape=out_shape,
        compiler_params=pltpu.CompilerParams(
            dimension_semantics=("arbitrary", "arbitrary"), vmem_limit_bytes=VMEM_LIMIT),
        name="proj",
    )(x, cos_t, sin_t, g_mix, wt, wg, gq, gk)


def _flash_t(q_t, k_block, v_block_t, acc_ref, n_blocks):
    nq = q_t.shape[1]
    acc_ref[...] = jnp.zeros_like(acc_ref)

    def body(j, carry):
        m, l = carry
        s = jnp.dot(k_block(j), q_t, preferred_element_type=F32)
        m_new = jnp.maximum(m, jnp.max(s, axis=0, keepdims=True))
        alpha = jnp.exp(m - m_new)
        p = jnp.exp(s - m_new)
        l_new = alpha * l + jnp.sum(p, axis=0, keepdims=True)
        pv = jnp.dot(v_block_t(j), p.astype(BF16), preferred_element_type=F32)
        acc_ref[...] = alpha * acc_ref[...] + pv
        return m_new, l_new

    m0 = jnp.full((1, nq), NEG_BIG, F32)
    l0 = jnp.zeros((1, nq), F32)
    return lax.fori_loop(0, n_blocks, body, (m0, l0))


def _attn_a_kernel(lq1_ref, lk1_ref, lq2_ref, lk2_ref, gd_ref, qt_ref, k_ref, vt_ref,
                   o_ref, acc_ref, *, tk):
    q = qt_ref[0]
    tq = q.shape[1]
    row = lax.broadcasted_iota(jnp.int32, q.shape, 0)
    zero = jnp.zeros_like(q)
    q_t = jnp.concatenate([jnp.where(row < A_HEAD_DIM, q, zero),
                           jnp.where(row >= A_HEAD_DIM, q, zero)], axis=1)
    n_blocks = k_ref.shape[1] // tk

    _, l = _flash_t(
        q_t,
        lambda j: k_ref[0, pl.ds(pl.multiple_of(j * tk, tk), tk), :],
        lambda j: vt_ref[0, j],
        acc_ref, n_blocks)

    lam = (jnp.exp(jnp.sum(lq1_ref[...] * lk1_ref[...], axis=-1, keepdims=True))
           - jnp.exp(jnp.sum(lq2_ref[...] * lk2_ref[...], axis=-1, keepdims=True))
           + LAM_INIT)
    acc = acc_ref[...]
    o = acc[:, :tq] / l[:, :tq] - lam * (acc[:, tq:] / l[:, tq:])
    y = _rms_rows(o, gd_ref[...]) * (1.0 - LAM_INIT)
    o_ref[0] = y.T.astype(BF16)


def _attn_b_kernel(qt_ref, k_ref, vt_ref, o_ref, acc_ref, *, tk):
    q = qt_ref[0]
    tq = q.shape[1]
    q_cat = jnp.concatenate([q[r * 64:(r + 1) * 64] for r in range(B_REP)], axis=1)
    q_t = jnp.concatenate([q_cat, jnp.zeros_like(q_cat)], axis=0)
    n_blocks = k_ref.shape[2] // tk

    _, l = _flash_t(
        q_t,
        lambda j: k_ref[0, 0, pl.ds(pl.multiple_of(j * tk, tk), tk), :],
        lambda j: vt_ref[0, j],
        acc_ref, n_blocks)

    o = acc_ref[...] / l
    o_rows = jnp.concatenate([o[:, r * tq:(r + 1) * tq] for r in range(B_REP)], axis=0)
    o_ref[0] = o_rows.T.astype(BF16)


def _attention_a(lq1, lk1, lq2, lk2, g_diff, qat, ka, vat):
    b, _, s = qat.shape
    tk = vat.shape[3]
    tq = min(A_Q_TILE, s)
    small = lambda shape: pl.BlockSpec(shape, lambda bi, h, i: (0,) * len(shape))
    return pl.pallas_call(
        functools.partial(_attn_a_kernel, tk=tk),
        grid=(b, A_HEADS, s // tq),
        in_specs=[
            small((1, A_HEAD_DIM)), small((1, A_HEAD_DIM)), small((1, A_HEAD_DIM)), small((1, A_HEAD_DIM)),
            small((2 * A_HEAD_DIM, 1)),
            pl.BlockSpec((1, 2 * A_HEAD_DIM, tq), lambda bi, h, i: (bi, h, i)),
            pl.BlockSpec((1, s, 2 * A_HEAD_DIM), lambda bi, h, i: (bi, 0, h)),
            pl.BlockSpec((1, s // tk, 2 * A_HEAD_DIM, tk), lambda bi, h, i: (bi, 0, h, 0)),
        ],
        out_specs=pl.BlockSpec((1, tq, 2 * A_HEAD_DIM), lambda bi, h, i: (bi, i, h)),
        out_shape=jax.ShapeDtypeStruct((b, s, A_W), BF16),
        scratch_shapes=[pltpu.VMEM((2 * A_HEAD_DIM, 2 * tq), F32)],
        compiler_params=pltpu.CompilerParams(
            dimension_semantics=("arbitrary", "arbitrary", "arbitrary"), vmem_limit_bytes=VMEM_LIMIT),
        name="attn_a",
    )(lq1, lk1, lq2, lk2, g_diff, qat, ka, vat)


def _attention_b(qbt, kb, vbt):
    b, _, s = qbt.shape
    tk = vbt.shape[3]
    tq = min(B_Q_TILE, s)
    rows = B_REP * B_HEAD_DIM
    return pl.pallas_call(
        functools.partial(_attn_b_kernel, tk=tk),
        grid=(b, B_KV_HEADS, s // tq),
        in_specs=[
            pl.BlockSpec((1, rows, tq), lambda bi, g, i: (bi, g, i)),
            pl.BlockSpec((1, 1, s, LANES), lambda bi, g, i: (bi, g, 0, 0)),
            pl.BlockSpec((1, s // tk, B_HEAD_DIM, tk), lambda bi, g, i: (bi, 0, g, 0)),
        ],
        out_specs=pl.BlockSpec((1, tq, rows), lambda bi, g, i: (bi, i, g)),
        out_shape=jax.ShapeDtypeStruct((b, s, B_Q), BF16),
        scratch_shapes=[pltpu.VMEM((B_HEAD_DIM, B_REP * tq), F32)],
        compiler_params=pltpu.CompilerParams(
            dimension_semantics=("arbitrary", "arbitrary", "arbitrary"), vmem_limit_bytes=VMEM_LIMIT),
        name="attn_b",
    )(qbt, kb, vbt)


def _merge_kernel(x_ref, oa_ref, ob_ref, ga_ref, gb_ref, wa_ref, wb_ref, wo_ref, h_ref):
    ya = jnp.dot(oa_ref[0], wa_ref[...], preferred_element_type=F32)
    yb = jnp.dot(ob_ref[0], wb_ref[...], preferred_element_type=F32)
    merged = ga_ref[0] * ya + gb_ref[0] * yb
    h_ref[0] = x_ref[0] + jnp.dot(merged.astype(BF16), wo_ref[...], preferred_element_type=F32)


def _merge(x, oa, ob, ga, gb, wa, wb, wo):
    b, s, d = x.shape
    t = min(MERGE_TILE, s)
    tile = pl.BlockSpec((1, t, d), lambda bi, i: (bi, i, 0))
    return pl.pallas_call(
        _merge_kernel,
        grid=(b, s // t),
        in_specs=[tile, tile, tile, tile, tile,
                  _const_spec(wa.shape), _const_spec(wb.shape), _const_spec(wo.shape)],
        out_specs=tile,
        out_shape=jax.ShapeDtypeStruct((b, s, d), F32),
        compiler_params=pltpu.CompilerParams(
            dimension_semantics=("arbitrary", "arbitrary"), vmem_limit_bytes=VMEM_LIMIT),
        name="merge",
    )(x, oa, ob, ga, gb, wa, wb, wo)


def _ffn_kernel(hp_ref, h_ref, hn_ref, p_ref, gffn_ref, wup_ref, cw_ref, cb_ref, wdn_ref,
                wple_ref, gple_ref, wpg_ref, gfin_ref, y_ref):
    h = h_ref[0]
    t = h.shape[0]
    first_row = pl.program_id(1) * t
    seq_len = pl.num_programs(1) * t
    hh = jnp.concatenate([hp_ref[0], h, hn_ref[0]], axis=0)
    n2 = _rms(hh, gffn_ref[...]).astype(BF16)
    rows = t + 2 * HALO

    r = lax.broadcasted_iota(jnp.int32, (t, 1), 0) + first_row
    keep_prev = r > 0
    keep_next = r < seq_len - 1

    def conv(u, c0):
        w = cw_ref[:, c0:c0 + FFN_CHUNK]
        um = pltpu.roll(u, 1, 0)[HALO:HALO + t]
        up = pltpu.roll(u, rows - 1, 0)[HALO:HALO + t]
        uc = u[HALO:HALO + t]
        um = jnp.where(keep_prev, um, 0.0)
        up = jnp.where(keep_next, up, 0.0)
        return um * w[0:1] + uc * w[1:2] + up * w[2:3] + cb_ref[:, c0:c0 + FFN_CHUNK]

    acc = jnp.zeros((t, D_MODEL), F32)
    for c in range(D_FF // FFN_CHUNK):
        c0 = c * FFN_CHUNK
        ug = jnp.dot(n2, wup_ref[:, c0:c0 + FFN_CHUNK], preferred_element_type=F32)
        uv = jnp.dot(n2, wup_ref[:, D_FF + c0:D_FF + c0 + FFN_CHUNK], preferred_element_type=F32)
        a = jax.nn.gelu(conv(ug, c0)) * conv(uv, D_FF + c0)
        acc = acc + jnp.dot(a.astype(BF16), wdn_ref[c0:c0 + FFN_CHUNK, :], preferred_element_type=F32)
    h2 = h + acc

    n3 = _rms(h2, gple_ref[...]).astype(BF16)
    gate = jax.nn.sigmoid(jnp.dot(n3, wpg_ref[...], preferred_element_type=F32))
    e = jnp.dot(p_ref[0].astype(BF16), wple_ref[...], preferred_element_type=F32)
    h3 = h2 + e * gate
    y_ref[0] = _rms(h3, gfin_ref[...])


def _ffn(h, p, g_ffn, wup, conv_w, conv_b, wdn, wple, g_ple, wpg, g_final):
    b, s, d = h.shape
    t = min(FFN_TILE, s)
    per = t // HALO
    nhb = s // HALO
    tile = pl.BlockSpec((1, t, d), lambda bi, i: (bi, i, 0))
    return pl.pallas_call(
        _ffn_kernel,
        grid=(b, s // t),
        in_specs=[
            pl.BlockSpec((1, HALO, d), lambda bi, i: (bi, jnp.maximum(i * per - 1, 0), 0)),
            tile,
            pl.BlockSpec((1, HALO, d), lambda bi, i: (bi, jnp.minimum((i + 1) * per, nhb - 1), 0)),
            pl.BlockSpec((1, t, PLE_DIM), lambda bi, i: (bi, i, 0)),
            _const_spec((1, d)),
            _const_spec(wup.shape),
            _const_spec(conv_w.shape),
            _const_spec(conv_b.shape),
            _const_spec(wdn.shape),
            _const_spec(wple.shape),
            _const_spec((1, d)),
            _const_spec(wpg.shape),
            _const_spec((1, d)),
        ],
        out_specs=tile,
        out_shape=jax.ShapeDtypeStruct((b, s, d), F32),
        compiler_params=pltpu.CompilerParams(
            dimension_semantics=("arbitrary", "arbitrary"), vmem_limit_bytes=VMEM_LIMIT),
        name="ffn",
    )(h, h, h, p, g_ffn, wup, conv_w, conv_b, wdn, wple, g_ple, wpg, g_final)


def _rope_tables_t(s):
    pos = jnp.arange(s, dtype=jnp.int32)

    def tab(ids, dim, theta):
        inv = theta ** (-jnp.arange(0, dim, 2, dtype=F32) / dim)
        ang = ids.astype(F32)[:, None] * inv[None, :]
        return jnp.cos(ang).T, jnp.sin(ang).T

    ca, sa = tab(pos, A_ROT_DIM, A_ROPE_THETA)
    cr, sr = tab(pos // GRID_W, B_HEAD_DIM // 2, B_AXIAL_THETA)
    cc, sc = tab(pos % GRID_W, B_HEAD_DIM // 2, B_AXIAL_THETA)
    return jnp.concatenate([ca, cr, cc], axis=0), jnp.concatenate([sa, sr, sc], axis=0)


def _encode(x, p, cos_t, sin_t, w):
    qat, ka, vat, qbt, kb, vbt, ga, gb = _project(
        x, cos_t, sin_t, w["g_mix"], w["wt"], w["wg"], w["g_qn"], w["g_kn"])
    oa = _attention_a(w["lq1"], w["lk1"], w["lq2"], w["lk2"], w["g_diff"], qat, ka, vat)
    ob = _attention_b(qbt, kb, vbt)
    h1 = _merge(x, oa, ob, ga, gb, w["w_a"], w["w_b"], w["w_out"])
    return _ffn(h1, p, w["g_ffn"], w["w_up"], w["conv_w"], w["conv_b"], w["w_down"],
                w["w_ple"], w["g_ple"], w["w_ple_gate"], w["g_final"])


def kernel(x_prompt, x_sample, p_prompt, p_sample, g_mix, w_in, lambda_q1, lambda_k1, lambda_q2,
           lambda_k2, g_diff, w_a, g_qn, g_kn, w_b, w_out, g_ffn, w_up, conv_w, conv_b, w_down,
           w_ple, g_ple, w_ple_gate, g_final):
    w_in0 = w_in[0]
    row = lambda v: v.reshape(1, -1).astype(F32)
    col = lambda v: v.reshape(-1, 1).astype(F32)
    w = dict(
        g_mix=row(g_mix[0]),
        wt=w_in0[:, :N_TRANSPOSED].T.astype(BF16),
        wg=w_in0[:, N_TRANSPOSED:].astype(BF16),
        g_qn=col(g_qn[0]), g_kn=col(g_kn[0]),
        lq1=row(lambda_q1[0]), lk1=row(lambda_k1[0]), lq2=row(lambda_q2[0]), lk2=row(lambda_k2[0]),
        g_diff=col(g_diff[0]),
        w_a=w_a[0].astype(BF16), w_b=w_b[0].astype(BF16), w_out=w_out[0].astype(BF16),
        g_ffn=row(g_ffn[0]),
        w_up=w_up[0].astype(BF16), conv_w=conv_w[0].astype(F32), conv_b=row(conv_b[0]),
        w_down=w_down[0].astype(BF16),
        w_ple=w_ple[0].astype(BF16), g_ple=row(g_ple[0]), w_ple_gate=w_ple_gate[0].astype(BF16),
        g_final=row(g_final),
    )
    outs = []
    for x, p in ((x_prompt, p_prompt), (x_sample, p_sample)):
        cos_t, sin_t = _rope_tables_t(x.shape[1])
        outs.append(_encode(x, p[0], cos_t, sin_t, w))
    return tuple(outs)
```

```python
import functools
import math

import jax
import jax.numpy as jnp
from jax import lax
from jax.experimental import pallas as pl
from jax.experimental.pallas import tpu as pltpu

F32 = jnp.float32
BF16 = jnp.bfloat16

D_MODEL = 1024
PLE_DIM = 256
GRID_W = 64
EPS = 1e-6
A_HEADS = 8
A_HEAD_DIM = 64
A_ROT_DIM = A_HEAD_DIM // 4
A_ROPE_THETA = 500000.0
B_HEADS = 16
B_KV_HEADS = 4
B_HEAD_DIM = 64
B_REP = B_HEADS // B_KV_HEADS
B_AXIAL_THETA = 10000.0
D_FF = 2816
LAM_INIT = 0.8 - 0.6 * math.exp(-0.3 * 0)

A_W = A_HEADS * 2 * A_HEAD_DIM
B_Q = B_HEADS * B_HEAD_DIM
B_KV = B_KV_HEADS * B_HEAD_DIM
N_TRANSPOSED = 3 * A_W + B_Q + 2 * B_KV

V7X_VMEM_BYTES = 64 * 1024 * 1024
VMEM_LIMIT = 56 * 1024 * 1024
SUBLANES = 8
LANES = 128

TOKEN_TILE = 512
A_Q_TILE = 512
B_Q_TILE = 256
MERGE_TILE = 512
FFN_TILE = 512
FFN_CHUNK = 256
HALO = SUBLANES
NEG_BIG = -1e30


def _rms(x, g):
    ms = jnp.mean(x * x, axis=-1, keepdims=True)
    return (x * lax.rsqrt(ms + EPS)) * g


def _rms_rows(x, g):
    ms = jnp.mean(x * x, axis=0, keepdims=True)
    return (x * lax.rsqrt(ms + EPS)) * g


def _proj_kernel(x_ref, cos_ref, sin_ref, gmix_ref, wt_ref, wg_ref, gq_ref, gk_ref,
                 qat_ref, ka_ref, vat_ref, qbt_ref, kb_ref, vbt_ref, ga_ref, gb_ref):
    x = x_ref[0]
    n = _rms(x, gmix_ref[...])
    nb = n.astype(BF16)
    nt = n.T.astype(BF16)

    cos = cos_ref[...]
    sin = sin_ref[...]
    ca, sa = cos[0:8], sin[0:8]
    cr, sr = cos[8:24], sin[8:24]
    cc, sc = cos[24:40], sin[24:40]

    def rope_a(p):
        x1, x2 = p[0:8], p[8:16]
        return jnp.concatenate([x1 * ca - x2 * sa, x2 * ca + x1 * sa, p[16:64]], axis=0)

    def rope_b(p):
        a1, a2, b1, b2 = p[0:16], p[16:32], p[32:48], p[48:64]
        return jnp.concatenate([a1 * cr - a2 * sr, a2 * cr + a1 * sr,
                                b1 * cc - b2 * sc, b2 * cc + b1 * sc], axis=0)

    def proj_t(lo, hi):
        return jnp.dot(wt_ref[lo:hi, :], nt, preferred_element_type=F32)

    scale = A_HEAD_DIM ** -0.5 * math.log2(math.e)

    p = proj_t(0, A_W)
    for g in range(A_W // 64):
        qat_ref[0, g * 64:(g + 1) * 64, :] = (rope_a(p[g * 64:(g + 1) * 64]) * scale).astype(BF16)

    p = proj_t(A_W, 2 * A_W)
    for s in range(A_W // LANES):
        blk = jnp.concatenate([rope_a(p[s * 128:s * 128 + 64]),
                               rope_a(p[s * 128 + 64:(s + 1) * 128])], axis=0)
        ka_ref[0, :, s * LANES:(s + 1) * LANES] = blk.T.astype(BF16)

    vat_ref[0, 0] = proj_t(2 * A_W, 3 * A_W).astype(BF16)

    off = 3 * A_W
    p = proj_t(off, off + B_Q)
    gq = gq_ref[...]
    for h in range(B_HEADS):
        y = rope_b(_rms_rows(p[h * 64:(h + 1) * 64], gq))
        qbt_ref[0, h * 64:(h + 1) * 64, :] = (y * scale).astype(BF16)

    off += B_Q
    p = proj_t(off, off + B_KV)
    gk = gk_ref[...]
    zeros = jnp.zeros((LANES - B_HEAD_DIM, p.shape[1]), F32)
    for g in range(B_KV_HEADS):
        y = rope_b(_rms_rows(p[g * 64:(g + 1) * 64], gk))
        kb_ref[0, g] = jnp.concatenate([y, zeros], axis=0).T.astype(BF16)

    off += B_KV
    vbt_ref[0, 0] = proj_t(off, off + B_KV).astype(BF16)

    gates = jnp.dot(nb, wg_ref[...], preferred_element_type=F32)
    ga_ref[0] = jax.nn.sigmoid(gates[:, :D_MODEL])
    gb_ref[0] = jax.nn.sigmoid(gates[:, D_MODEL:])


def _const_spec(shape):
    nd = len(shape)
    return pl.BlockSpec(shape, lambda *_: (0,) * nd, pipeline_mode=pl.Buffered(1))


def _project(x, cos_t, sin_t, g_mix, wt, wg, gq, gk):
    b, s, d = x.shape
    t = min(TOKEN_TILE, s)
    nt = s // t
    out_shape = (
        jax.ShapeDtypeStruct((b, A_W, s), BF16),
        jax.ShapeDtypeStruct((b, s, A_W), BF16),
        jax.ShapeDtypeStruct((b, nt, A_W, t), BF16),
        jax.ShapeDtypeStruct((b, B_Q, s), BF16),
        jax.ShapeDtypeStruct((b, B_KV_HEADS, s, LANES), BF16),
        jax.ShapeDtypeStruct((b, nt, B_KV, t), BF16),
        jax.ShapeDtypeStruct((b, s, D_MODEL), F32),
        jax.ShapeDtypeStruct((b, s, D_MODEL), F32),
    )
    in_specs = [
        pl.BlockSpec((1, t, d), lambda bi, i: (bi, i, 0)),
        pl.BlockSpec((40, t), lambda bi, i: (0, i)),
        pl.BlockSpec((40, t), lambda bi, i: (0, i)),
        _const_spec((1, d)),
        _const_spec(wt.shape),
        _const_spec(wg.shape),
        _const_spec((B_HEAD_DIM, 1)),
        _const_spec((B_HEAD_DIM, 1)),
    ]
    out_specs = (
        pl.BlockSpec((1, A_W, t), lambda bi, i: (bi, 0, i)),
        pl.BlockSpec((1, t, A_W), lambda bi, i: (bi, i, 0)),
        pl.BlockSpec((1, 1, A_W, t), lambda bi, i: (bi, i, 0, 0)),
        pl.BlockSpec((1, B_Q, t), lambda bi, i: (bi, 0, i)),
        pl.BlockSpec((1, B_KV_HEADS, t, LANES), lambda bi, i: (bi, 0, i, 0)),
        pl.BlockSpec((1, 1, B_KV, t), lambda bi, i: (bi, i, 0, 0)),
        pl.BlockSpec((1, t, D_MODEL), lambda bi, i: (bi, i, 0)),
        pl.BlockSpec((1, t, D_MODEL), lambda bi, i: (bi, i, 0)),
    )
    return pl.pallas_call(
        _proj_kernel,
        grid=(b, nt),
        in_specs=in_specs,
        out_specs=out_specs,
        out_shape=out_shape,
        compiler_params=pltpu.CompilerParams(
            dimension_semantics=("arbitrary", "arbitrary"), vmem_limit_bytes=VMEM_LIMIT),
        name="proj",
    )(x, cos_t, sin_t, g_mix, wt, wg, gq, gk)


SUM_ROWS = 16


def _flash_t(q_t, k_block, v_block_t, s_refs, acc_ref, n_blocks):
    nq = q_t.shape[1]
    s_a, s_b = s_refs
    tk = s_a.shape[0]
    ones = jnp.concatenate([jnp.ones((1, tk), BF16), jnp.zeros((SUM_ROWS - 1, tk), BF16)], axis=0)
    acc_ref[...] = jnp.zeros_like(acc_ref)

    def scores(j, s_ref):
        s = jnp.dot(k_block(j), q_t, preferred_element_type=F32)
        s_ref[...] = s
        return jnp.max(s, axis=0, keepdims=True)

    def consume(j, s_ref, m, bmax):
        m_new = jnp.maximum(m, bmax)
        alpha = jnp.exp2(m - m_new)
        p = jnp.exp2(s_ref[...] - m_new).astype(BF16)
        v_ext = jnp.concatenate([v_block_t(j), ones], axis=0)
        pv = jnp.dot(v_ext, p, preferred_element_type=F32)
        acc_ref[...] = alpha * acc_ref[...] + pv
        return m_new

    def step(j, s_cur, s_nxt, m, bmax):
        bmax_nxt = scores(j + 1, s_nxt)
        return consume(j, s_cur, m, bmax), bmax_nxt

    def body(jj, carry):
        m, bmax = carry
        m, bmax = step(2 * jj, s_a, s_b, m, bmax)
        return step(2 * jj + 1, s_b, s_a, m, bmax)

    m = jnp.full((1, nq), NEG_BIG, F32)
    bmax = scores(0, s_a)
    m, bmax = lax.fori_loop(0, n_blocks // 2 - 1, body, (m, bmax))
    m, bmax = step(n_blocks - 2, s_a, s_b, m, bmax)
    consume(n_blocks - 1, s_b, m, bmax)


def _attn_a_kernel(lq1_ref, lk1_ref, lq2_ref, lk2_ref, gd_ref, qt_ref, k_ref, vt_ref,
                   o_ref, s_a, s_b, acc_ref, *, tk):
    q = qt_ref[0]
    tq = q.shape[1]
    row = lax.broadcasted_iota(jnp.int32, q.shape, 0)
    zero = jnp.zeros_like(q)
    q_t = jnp.concatenate([jnp.where(row < A_HEAD_DIM, q, zero),
                           jnp.where(row >= A_HEAD_DIM, q, zero)], axis=1)
    n_blocks = k_ref.shape[1] // tk

    _flash_t(
        q_t,
        lambda j: k_ref[0, pl.ds(pl.multiple_of(j * tk, tk), tk), :],
        lambda j: vt_ref[0, j],
        (s_a, s_b), acc_ref, n_blocks)

    lam = (jnp.exp(jnp.sum(lq1_ref[...] * lk1_ref[...], axis=-1, keepdims=True))
           - jnp.exp(jnp.sum(lq2_ref[...] * lk2_ref[...], axis=-1, keepdims=True))
           + LAM_INIT)
    dv = 2 * A_HEAD_DIM
    acc = acc_ref[0:dv, :]
    l = acc_ref[dv:dv + 1, :]
    o = acc[:, :tq] / l[:, :tq] - lam * (acc[:, tq:] / l[:, tq:])
    y = _rms_rows(o, gd_ref[...]) * (1.0 - LAM_INIT)
    o_ref[0] = y.T.astype(BF16)


def _attn_b_kernel(qt_ref, k_ref, vt_ref, o_ref, s_a, s_b, acc_ref, *, tk):
    q = qt_ref[0]
    tq = q.shape[1]
    q_cat = jnp.concatenate([q[r * 64:(r + 1) * 64] for r in range(B_REP)], axis=1)
    q_t = jnp.concatenate([q_cat, jnp.zeros_like(q_cat)], axis=0)
    n_blocks = k_ref.shape[2] // tk

    _flash_t(
        q_t,
        lambda j: k_ref[0, 0, pl.ds(pl.multiple_of(j * tk, tk), tk), :],
        lambda j: vt_ref[0, j],
        (s_a, s_b), acc_ref, n_blocks)

    o = acc_ref[0:B_HEAD_DIM, :] / acc_ref[B_HEAD_DIM:B_HEAD_DIM + 1, :]
    o_rows = jnp.concatenate([o[:, r * tq:(r + 1) * tq] for r in range(B_REP)], axis=0)
    o_ref[0] = o_rows.T.astype(BF16)


def _attention_a(lq1, lk1, lq2, lk2, g_diff, qat, ka, vat):
    b, _, s = qat.shape
    tk = vat.shape[3]
    tq = min(A_Q_TILE, s)
    small = lambda shape: pl.BlockSpec(shape, lambda bi, h, i: (0,) * len(shape))
    return pl.pallas_call(
        functools.partial(_attn_a_kernel, tk=tk),
        grid=(b, A_HEADS, s // tq),
        in_specs=[
            small((1, A_HEAD_DIM)), small((1, A_HEAD_DIM)), small((1, A_HEAD_DIM)), small((1, A_HEAD_DIM)),
            small((2 * A_HEAD_DIM, 1)),
            pl.BlockSpec((1, 2 * A_HEAD_DIM, tq), lambda bi, h, i: (bi, h, i)),
            pl.BlockSpec((1, s, 2 * A_HEAD_DIM), lambda bi, h, i: (bi, 0, h)),
            pl.BlockSpec((1, s // tk, 2 * A_HEAD_DIM, tk), lambda bi, h, i: (bi, 0, h, 0)),
        ],
        out_specs=pl.BlockSpec((1, tq, 2 * A_HEAD_DIM), lambda bi, h, i: (bi, i, h)),
        out_shape=jax.ShapeDtypeStruct((b, s, A_W), BF16),
        scratch_shapes=[pltpu.VMEM((tk, 2 * tq), F32), pltpu.VMEM((tk, 2 * tq), F32),
                        pltpu.VMEM((2 * A_HEAD_DIM + SUM_ROWS, 2 * tq), F32)],
        compiler_params=pltpu.CompilerParams(
            dimension_semantics=("arbitrary", "arbitrary", "arbitrary"), vmem_limit_bytes=VMEM_LIMIT),
        name="attn_a",
    )(lq1, lk1, lq2, lk2, g_diff, qat, ka, vat)


def _attention_b(qbt, kb, vbt):
    b, _, s = qbt.shape
    tk = vbt.shape[3]
    tq = min(B_Q_TILE, s)
    rows = B_REP * B_HEAD_DIM
    return pl.pallas_call(
        functools.partial(_attn_b_kernel, tk=tk),
        grid=(b, B_KV_HEADS, s // tq),
        in_specs=[
            pl.BlockSpec((1, rows, tq), lambda bi, g, i: (bi, g, i)),
            pl.BlockSpec((1, 1, s, LANES), lambda bi, g, i: (bi, g, 0, 0)),
            pl.BlockSpec((1, s // tk, B_HEAD_DIM, tk), lambda bi, g, i: (bi, 0, g, 0)),
        ],
        out_specs=pl.BlockSpec((1, tq, rows), lambda bi, g, i: (bi, i, g)),
        out_shape=jax.ShapeDtypeStruct((b, s, B_Q), BF16),
        scratch_shapes=[pltpu.VMEM((tk, B_REP * tq), F32), pltpu.VMEM((tk, B_REP * tq), F32),
                        pltpu.VMEM((B_HEAD_DIM + SUM_ROWS, B_REP * tq), F32)],
        compiler_params=pltpu.CompilerParams(
            dimension_semantics=("arbitrary", "arbitrary", "arbitrary"), vmem_limit_bytes=VMEM_LIMIT),
        name="attn_b",
    )(qbt, kb, vbt)


def _merge_kernel(x_ref, oa_ref, ob_ref, ga_ref, gb_ref, wa_ref, wb_ref, wo_ref, h_ref):
    ya = jnp.dot(oa_ref[0], wa_ref[...], preferred_element_type=F32)
    yb = jnp.dot(ob_ref[0], wb_ref[...], preferred_element_type=F32)
    merged = ga_ref[0] * ya + gb_ref[0] * yb
    h_ref[0] = x_ref[0] + jnp.dot(merged.astype(BF16), wo_ref[...], preferred_element_type=F32)


def _merge(x, oa, ob, ga, gb, wa, wb, wo):
    b, s, d = x.shape
    t = min(MERGE_TILE, s)
    tile = pl.BlockSpec((1, t, d), lambda bi, i: (bi, i, 0))
    return pl.pallas_call(
        _merge_kernel,
        grid=(b, s // t),
        in_specs=[tile, tile, tile, tile, tile,
                  _const_spec(wa.shape), _const_spec(wb.shape), _const_spec(wo.shape)],
        out_specs=tile,
        out_shape=jax.ShapeDtypeStruct((b, s, d), F32),
        compiler_params=pltpu.CompilerParams(
            dimension_semantics=("arbitrary", "arbitrary"), vmem_limit_bytes=VMEM_LIMIT),
        name="merge",
    )(x, oa, ob, ga, gb, wa, wb, wo)


def _ffn_kernel(hp_ref, h_ref, hn_ref, p_ref, gffn_ref, wup_ref, cw_ref, cb_ref, wdn_ref,
                wple_ref, gple_ref, wpg_ref, gfin_ref, y_ref):
    h = h_ref[0]
    t = h.shape[0]
    first_row = pl.program_id(1) * t
    seq_len = pl.num_programs(1) * t
    hh = jnp.concatenate([hp_ref[0], h, hn_ref[0]], axis=0)
    n2 = _rms(hh, gffn_ref[...]).astype(BF16)
    rows = t + 2 * HALO

    r = lax.broadcasted_iota(jnp.int32, (t, 1), 0) + first_row
    keep_prev = r > 0
    keep_next = r < seq_len - 1

    def conv(u, c0):
        w = cw_ref[:, c0:c0 + FFN_CHUNK]
        um = pltpu.roll(u, 1, 0)[HALO:HALO + t]
        up = pltpu.roll(u, rows - 1, 0)[HALO:HALO + t]
        uc = u[HALO:HALO + t]
        um = jnp.where(keep_prev, um, 0.0)
        up = jnp.where(keep_next, up, 0.0)
        return um * w[0:1] + uc * w[1:2] + up * w[2:3] + cb_ref[:, c0:c0 + FFN_CHUNK]

    acc = jnp.zeros((t, D_MODEL), F32)
    for c in range(D_FF // FFN_CHUNK):
        c0 = c * FFN_CHUNK
        ug = jnp.dot(n2, wup_ref[:, c0:c0 + FFN_CHUNK], preferred_element_type=F32)
        uv = jnp.dot(n2, wup_ref[:, D_FF + c0:D_FF + c0 + FFN_CHUNK], preferred_element_type=F32)
        a = jax.nn.gelu(conv(ug, c0)) * conv(uv, D_FF + c0)
        acc = acc + jnp.dot(a.astype(BF16), wdn_ref[c0:c0 + FFN_CHUNK, :], preferred_element_type=F32)
    h2 = h + acc

    n3 = _rms(h2, gple_ref[...]).astype(BF16)
    gate = jax.nn.sigmoid(jnp.dot(n3, wpg_ref[...], preferred_element_type=F32))
    e = jnp.dot(p_ref[0].astype(BF16), wple_ref[...], preferred_element_type=F32)
    h3 = h2 + e * gate
    y_ref[0] = _rms(h3, gfin_ref[...])


def _ffn(h, p, g_ffn, wup, conv_w, conv_b, wdn, wple, g_ple, wpg, g_final):
    b, s, d = h.shape
    t = min(FFN_TILE, s)
    per = t // HALO
    nhb = s // HALO
    tile = pl.BlockSpec((1, t, d), lambda bi, i: (bi, i, 0))
    return pl.pallas_call(
        _ffn_kernel,
        grid=(b, s // t),
        in_specs=[
            pl.BlockSpec((1, HALO, d), lambda bi, i: (bi, jnp.maximum(i * per - 1, 0), 0)),
            tile,
            pl.BlockSpec((1, HALO, d), lambda bi, i: (bi, jnp.minimum((i + 1) * per, nhb - 1), 0)),
            pl.BlockSpec((1, t, PLE_DIM), lambda bi, i: (bi, i, 0)),
            _const_spec((1, d)),
            _const_spec(wup.shape),
            _const_spec(conv_w.shape),
            _const_spec(conv_b.shape),
            _const_spec(wdn.shape),
            _const_spec(wple.shape),
            _const_spec((1, d)),
            _const_spec(wpg.shape),
            _const_spec((1, d)),
        ],
        out_specs=tile,
        out_shape=jax.ShapeDtypeStruct((b, s, d), F32),
        compiler_params=pltpu.CompilerParams(
            dimension_semantics=("arbitrary", "arbitrary"), vmem_limit_bytes=VMEM_LIMIT),
        name="ffn",
    )(h, h, h, p, g_ffn, wup, conv_w, conv_b, wdn, wple, g_ple, wpg, g_final)


def _rope_tables_t(s):
    pos = jnp.arange(s, dtype=jnp.int32)

    def tab(ids, dim, theta):
        inv = theta ** (-jnp.arange(0, dim, 2, dtype=F32) / dim)
        ang = ids.astype(F32)[:, None] * inv[None, :]
        return jnp.cos(ang).T, jnp.sin(ang).T

    ca, sa = tab(pos, A_ROT_DIM, A_ROPE_THETA)
    cr, sr = tab(pos // GRID_W, B_HEAD_DIM // 2, B_AXIAL_THETA)
    cc, sc = tab(pos % GRID_W, B_HEAD_DIM // 2, B_AXIAL_THETA)
    return jnp.concatenate([ca, cr, cc], axis=0), jnp.concatenate([sa, sr, sc], axis=0)


def _encode(x, p, cos_t, sin_t, w):
    qat, ka, vat, qbt, kb, vbt, ga, gb = _project(
        x, cos_t, sin_t, w["g_mix"], w["wt"], w["wg"], w["g_qn"], w["g_kn"])
    oa = _attention_a(w["lq1"], w["lk1"], w["lq2"], w["lk2"], w["g_diff"], qat, ka, vat)
    ob = _attention_b(qbt, kb, vbt)
    h1 = _merge(x, oa, ob, ga, gb, w["w_a"], w["w_b"], w["w_out"])
    return _ffn(h1, p, w["g_ffn"], w["w_up"], w["conv_w"], w["conv_b"], w["w_down"],
                w["w_ple"], w["g_ple"], w["w_ple_gate"], w["g_final"])


def kernel(x_prompt, x_sample, p_prompt, p_sample, g_mix, w_in, lambda_q1, lambda_k1, lambda_q2,
           lambda_k2, g_diff, w_a, g_qn, g_kn, w_b, w_out, g_ffn, w_up, conv_w, conv_b, w_down,
           w_ple, g_ple, w_ple_gate, g_final):
    w_in0 = w_in[0]
    row = lambda v: v.reshape(1, -1).astype(F32)
    col = lambda v: v.reshape(-1, 1).astype(F32)
    w = dict(
        g_mix=row(g_mix[0]),
        wt=w_in0[:, :N_TRANSPOSED].T.astype(BF16),
        wg=w_in0[:, N_TRANSPOSED:].astype(BF16),
        g_qn=col(g_qn[0]), g_kn=col(g_kn[0]),
        lq1=row(lambda_q1[0]), lk1=row(lambda_k1[0]), lq2=row(lambda_q2[0]), lk2=row(lambda_k2[0]),
        g_diff=col(g_diff[0]),
        w_a=w_a[0].astype(BF16), w_b=w_b[0].astype(BF16), w_out=w_out[0].astype(BF16),
        g_ffn=row(g_ffn[0]),
        w_up=w_up[0].astype(BF16), conv_w=conv_w[0].astype(F32), conv_b=row(conv_b[0]),
        w_down=w_down[0].astype(BF16),
        w_ple=w_ple[0].astype(BF16), g_ple=row(g_ple[0]), w_ple_gate=w_ple_gate[0].astype(BF16),
        g_final=row(g_final),
    )
    outs = []
    for x, p in ((x_prompt, p_prompt), (x_sample, p_sample)):
        cos_t, sin_t = _rope_tables_t(x.shape[1])
        outs.append(_encode(x, p[0], cos_t, sin_t, w))
    return tuple(outs)
```

```python
import functools
import math

import jax
import jax.numpy as jnp
from jax import lax
from jax.experimental import pallas as pl
from jax.experimental.pallas import tpu as pltpu

F32 = jnp.float32
BF16 = jnp.bfloat16

D_MODEL = 1024
PLE_DIM = 256
GRID_W = 64
EPS = 1e-6
A_HEADS = 8
A_HEAD_DIM = 64
A_ROT_DIM = A_HEAD_DIM // 4
A_ROPE_THETA = 500000.0
B_HEADS = 16
B_KV_HEADS = 4
B_HEAD_DIM = 64
B_REP = B_HEADS // B_KV_HEADS
B_AXIAL_THETA = 10000.0
D_FF = 2816
LAM_INIT = 0.8 - 0.6 * math.exp(-0.3 * 0)

A_W = A_HEADS * 2 * A_HEAD_DIM
B_Q = B_HEADS * B_HEAD_DIM
B_KV = B_KV_HEADS * B_HEAD_DIM
N_TRANSPOSED = 3 * A_W + B_Q + 2 * B_KV

V7X_VMEM_BYTES = 64 * 1024 * 1024
VMEM_LIMIT = 56 * 1024 * 1024
SUBLANES = 8
LANES = 128

TOKEN_TILE = 512
A_Q_TILE = 512
B_Q_TILE = 256
MERGE_TILE = 512
FFN_TILE = 512
FFN_CHUNK = 256
HALO = SUBLANES
SUM_ROWS = 16
A_VROWS = 2 * A_HEAD_DIM + SUM_ROWS
B_VROWS = B_HEAD_DIM + SUM_ROWS
NEG_BIG = -1e30


def _rms(x, g):
    ms = jnp.mean(x * x, axis=-1, keepdims=True)
    return (x * lax.rsqrt(ms + EPS)) * g


def _rms_rows(x, g):
    ms = jnp.mean(x * x, axis=0, keepdims=True)
    return (x * lax.rsqrt(ms + EPS)) * g


def _proj_kernel(x_ref, cos_ref, sin_ref, gmix_ref, wt_ref, wg_ref, gq_ref, gk_ref,
                 qat_ref, ka_ref, vat_ref, qbt_ref, kb_ref, vbt_ref, ga_ref, gb_ref):
    x = x_ref[0]
    n = _rms(x, gmix_ref[...])
    nb = n.astype(BF16)
    nt = n.T.astype(BF16)

    cos = cos_ref[...]
    sin = sin_ref[...]
    ca, sa = cos[0:8], sin[0:8]
    cr, sr = cos[8:24], sin[8:24]
    cc, sc = cos[24:40], sin[24:40]

    def rope_a(p):
        x1, x2 = p[0:8], p[8:16]
        return jnp.concatenate([x1 * ca - x2 * sa, x2 * ca + x1 * sa, p[16:64]], axis=0)

    def rope_b(p):
        a1, a2, b1, b2 = p[0:16], p[16:32], p[32:48], p[48:64]
        return jnp.concatenate([a1 * cr - a2 * sr, a2 * cr + a1 * sr,
                                b1 * cc - b2 * sc, b2 * cc + b1 * sc], axis=0)

    def proj_t(lo, hi):
        return jnp.dot(wt_ref[lo:hi, :], nt, preferred_element_type=F32)

    scale = A_HEAD_DIM ** -0.5 * math.log2(math.e)

    p = proj_t(0, A_W)
    for g in range(A_W // 64):
        qat_ref[0, g * 64:(g + 1) * 64, :] = (rope_a(p[g * 64:(g + 1) * 64]) * scale).astype(BF16)

    p = proj_t(A_W, 2 * A_W)
    for s in range(A_W // LANES):
        blk = jnp.concatenate([rope_a(p[s * 128:s * 128 + 64]),
                               rope_a(p[s * 128 + 64:(s + 1) * 128])], axis=0)
        ka_ref[0, :, s * LANES:(s + 1) * LANES] = blk.T.astype(BF16)

    ones_rows = (lax.broadcasted_iota(jnp.int32, (SUM_ROWS, nt.shape[1]), 0) == 0).astype(BF16)
    p = proj_t(2 * A_W, 3 * A_W)
    dv = 2 * A_HEAD_DIM
    for h in range(A_HEADS):
        vat_ref[0, 0, h * A_VROWS:h * A_VROWS + dv, :] = p[h * dv:(h + 1) * dv].astype(BF16)
        vat_ref[0, 0, h * A_VROWS + dv:(h + 1) * A_VROWS, :] = ones_rows

    off = 3 * A_W
    p = proj_t(off, off + B_Q)
    gq = gq_ref[...]
    for h in range(B_HEADS):
        y = rope_b(_rms_rows(p[h * 64:(h + 1) * 64], gq))
        qbt_ref[0, h * 64:(h + 1) * 64, :] = (y * scale).astype(BF16)

    off += B_Q
    p = proj_t(off, off + B_KV)
    gk = gk_ref[...]
    zeros = jnp.zeros((LANES - B_HEAD_DIM, p.shape[1]), F32)
    for g in range(B_KV_HEADS):
        y = rope_b(_rms_rows(p[g * 64:(g + 1) * 64], gk))
        kb_ref[0, g] = jnp.concatenate([y, zeros], axis=0).T.astype(BF16)

    off += B_KV
    p = proj_t(off, off + B_KV)
    dv = B_HEAD_DIM
    for g in range(B_KV_HEADS):
        vbt_ref[0, 0, g * B_VROWS:g * B_VROWS + dv, :] = p[g * dv:(g + 1) * dv].astype(BF16)
        vbt_ref[0, 0, g * B_VROWS + dv:(g + 1) * B_VROWS, :] = ones_rows

    gates = jnp.dot(nb, wg_ref[...], preferred_element_type=F32)
    ga_ref[0] = jax.nn.sigmoid(gates[:, :D_MODEL])
    gb_ref[0] = jax.nn.sigmoid(gates[:, D_MODEL:])


def _const_spec(shape):
    nd = len(shape)
    return pl.BlockSpec(shape, lambda *_: (0,) * nd, pipeline_mode=pl.Buffered(1))


def _project(x, cos_t, sin_t, g_mix, wt, wg, gq, gk):
    b, s, d = x.shape
    t = min(TOKEN_TILE, s)
    nt = s // t
    out_shape = (
        jax.ShapeDtypeStruct((b, A_W, s), BF16),
        jax.ShapeDtypeStruct((b, s, A_W), BF16),
        jax.ShapeDtypeStruct((b, nt, A_HEADS * A_VROWS, t), BF16),
        jax.ShapeDtypeStruct((b, B_Q, s), BF16),
        jax.ShapeDtypeStruct((b, B_KV_HEADS, s, LANES), BF16),
        jax.ShapeDtypeStruct((b, nt, B_KV_HEADS * B_VROWS, t), BF16),
        jax.ShapeDtypeStruct((b, s, D_MODEL), F32),
        jax.ShapeDtypeStruct((b, s, D_MODEL), F32),
    )
    in_specs = [
        pl.BlockSpec((1, t, d), lambda bi, i: (bi, i, 0)),
        pl.BlockSpec((40, t), lambda bi, i: (0, i)),
        pl.BlockSpec((40, t), lambda bi, i: (0, i)),
        _const_spec((1, d)),
        _const_spec(wt.shape),
        _const_spec(wg.shape),
        _const_spec((B_HEAD_DIM, 1)),
        _const_spec((B_HEAD_DIM, 1)),
    ]
    out_specs = (
        pl.BlockSpec((1, A_W, t), lambda bi, i: (bi, 0, i)),
        pl.BlockSpec((1, t, A_W), lambda bi, i: (bi, i, 0)),
        pl.BlockSpec((1, 1, A_HEADS * A_VROWS, t), lambda bi, i: (bi, i, 0, 0)),
        pl.BlockSpec((1, B_Q, t), lambda bi, i: (bi, 0, i)),
        pl.BlockSpec((1, B_KV_HEADS, t, LANES), lambda bi, i: (bi, 0, i, 0)),
        pl.BlockSpec((1, 1, B_KV_HEADS * B_VROWS, t), lambda bi, i: (bi, i, 0, 0)),
        pl.BlockSpec((1, t, D_MODEL), lambda bi, i: (bi, i, 0)),
        pl.BlockSpec((1, t, D_MODEL), lambda bi, i: (bi, i, 0)),
    )
    return pl.pallas_call(
        _proj_kernel,
        grid=(b, nt),
        in_specs=in_specs,
        out_specs=out_specs,
        out_shape=out_shape,
        compiler_params=pltpu.CompilerParams(
            dimension_semantics=("arbitrary", "arbitrary"), vmem_limit_bytes=VMEM_LIMIT),
        name="proj",
    )(x, cos_t, sin_t, g_mix, wt, wg, gq, gk)


KEY_BLOCK = 256


def _flash_t(q_t, k_rows, v_tile_t, s_refs, acc_ref, n_tiles):
    s_a, s_b = s_refs
    tk = s_a.shape[0]
    acc_ref[...] = jnp.zeros_like(acc_ref)

    def scores(j, s_ref):
        s = jnp.dot(k_rows(pl.multiple_of(j * tk, tk), tk), q_t, preferred_element_type=F32)
        s_ref[...] = s
        return jnp.max(s, axis=0, keepdims=True)

    def consume(v_t, s_ref, m, bmax):
        m_new = jnp.maximum(m, bmax)
        alpha = jnp.exp2(m - m_new)
        p = jnp.exp2(s_ref[...] - m_new).astype(BF16)
        acc_ref[...] = alpha * acc_ref[...] + jnp.dot(v_t, p, preferred_element_type=F32)
        return m_new

    def tile(i, carry, last):
        m, bmax_a = carry
        v = v_tile_t(i)
        assert v.shape[1] == 2 * tk
        bmax_b = scores(2 * i + 1, s_b)
        m = consume(v[:, :tk], s_a, m, bmax_a)
        if not last:
            bmax_a = scores(2 * i + 2, s_a)
        m = consume(v[:, tk:], s_b, m, bmax_b)
        return m, bmax_a

    bmax = scores(0, s_a)
    carry = (jnp.full_like(bmax, NEG_BIG), bmax)
    n_loop = n_tiles - 1
    unroll = next(u for u in (5, 3, 1) if n_loop % u == 0)
    carry = lax.fori_loop(0, n_loop, lambda i, c: tile(i, c, False), carry, unroll=unroll)
    tile(n_tiles - 1, carry, True)


def _attn_a_kernel(lq1_ref, lk1_ref, lq2_ref, lk2_ref, gd_ref, qt_ref, k_ref, vt_ref,
                   o_ref, s_a, s_b, acc_ref):
    q = qt_ref[0]
    tq = q.shape[1]
    row = lax.broadcasted_iota(jnp.int32, q.shape, 0)
    zero = jnp.zeros_like(q)
    q_t = jnp.concatenate([jnp.where(row < A_HEAD_DIM, q, zero),
                           jnp.where(row >= A_HEAD_DIM, q, zero)], axis=1)
    _flash_t(q_t, lambda r0, n: k_ref[0, pl.ds(r0, n), :], lambda i: vt_ref[0, i],
             (s_a, s_b), acc_ref, vt_ref.shape[1])

    lam = (jnp.exp(jnp.sum(lq1_ref[...] * lk1_ref[...], axis=-1, keepdims=True))
           - jnp.exp(jnp.sum(lq2_ref[...] * lk2_ref[...], axis=-1, keepdims=True))
           + LAM_INIT)
    dv = 2 * A_HEAD_DIM
    acc = acc_ref[0:dv, :]
    l = acc_ref[dv:dv + 1, :]
    o = acc[:, :tq] / l[:, :tq] - lam * (acc[:, tq:] / l[:, tq:])
    y = _rms_rows(o, gd_ref[...]) * (1.0 - LAM_INIT)
    o_ref[0] = y.T.astype(BF16)


def _attn_b_kernel(qt_ref, k_ref, vt_ref, o_ref, s_a, s_b, acc_ref):
    q = qt_ref[0]
    tq = q.shape[1]
    q_cat = jnp.concatenate([q[r * 64:(r + 1) * 64] for r in range(B_REP)], axis=1)
    q_t = jnp.concatenate([q_cat, jnp.zeros_like(q_cat)], axis=0)
    _flash_t(q_t, lambda r0, n: k_ref[0, 0, pl.ds(r0, n), :], lambda i: vt_ref[0, i],
             (s_a, s_b), acc_ref, vt_ref.shape[1])

    o = acc_ref[0:B_HEAD_DIM, :] / acc_ref[B_HEAD_DIM:B_HEAD_DIM + 1, :]
    o_rows = jnp.concatenate([o[:, r * tq:(r + 1) * tq] for r in range(B_REP)], axis=0)
    o_ref[0] = o_rows.T.astype(BF16)


def _attention_a(lq1, lk1, lq2, lk2, g_diff, qat, ka, vat):
    b, _, s = qat.shape
    tk = vat.shape[3]
    tq = min(A_Q_TILE, s)
    small = lambda shape: pl.BlockSpec(shape, lambda bi, h, i: (0,) * len(shape))
    return pl.pallas_call(
        _attn_a_kernel,
        grid=(b, A_HEADS, s // tq),
        in_specs=[
            small((1, A_HEAD_DIM)), small((1, A_HEAD_DIM)), small((1, A_HEAD_DIM)), small((1, A_HEAD_DIM)),
            small((2 * A_HEAD_DIM, 1)),
            pl.BlockSpec((1, 2 * A_HEAD_DIM, tq), lambda bi, h, i: (bi, h, i)),
            pl.BlockSpec((1, s, 2 * A_HEAD_DIM), lambda bi, h, i: (bi, 0, h)),
            pl.BlockSpec((1, s // tk, A_VROWS, tk), lambda bi, h, i: (bi, 0, h, 0)),
        ],
        out_specs=pl.BlockSpec((1, tq, 2 * A_HEAD_DIM), lambda bi, h, i: (bi, i, h)),
        out_shape=jax.ShapeDtypeStruct((b, s, A_W), BF16),
        scratch_shapes=[pltpu.VMEM((KEY_BLOCK, 2 * tq), F32), pltpu.VMEM((KEY_BLOCK, 2 * tq), F32),
                        pltpu.VMEM((A_VROWS, 2 * tq), F32)],
        compiler_params=pltpu.CompilerParams(
            dimension_semantics=("arbitrary", "arbitrary", "arbitrary"), vmem_limit_bytes=VMEM_LIMIT),
        name="attn_a",
    )(lq1, lk1, lq2, lk2, g_diff, qat, ka, vat)


def _attention_b(qbt, kb, vbt):
    b, _, s = qbt.shape
    tk = vbt.shape[3]
    tq = min(B_Q_TILE, s)
    rows = B_REP * B_HEAD_DIM
    return pl.pallas_call(
        _attn_b_kernel,
        grid=(b, B_KV_HEADS, s // tq),
        in_specs=[
            pl.BlockSpec((1, rows, tq), lambda bi, g, i: (bi, g, i)),
            pl.BlockSpec((1, 1, s, LANES), lambda bi, g, i: (bi, g, 0, 0)),
            pl.BlockSpec((1, s // tk, B_VROWS, tk), lambda bi, g, i: (bi, 0, g, 0)),
        ],
        out_specs=pl.BlockSpec((1, tq, rows), lambda bi, g, i: (bi, i, g)),
        out_shape=jax.ShapeDtypeStruct((b, s, B_Q), BF16),
        scratch_shapes=[pltpu.VMEM((KEY_BLOCK, B_REP * tq), F32), pltpu.VMEM((KEY_BLOCK, B_REP * tq), F32),
                        pltpu.VMEM((B_VROWS, B_REP * tq), F32)],
        compiler_params=pltpu.CompilerParams(
            dimension_semantics=("arbitrary", "arbitrary", "arbitrary"), vmem_limit_bytes=VMEM_LIMIT),
        name="attn_b",
    )(qbt, kb, vbt)


def _merge_kernel(x_ref, oa_ref, ob_ref, ga_ref, gb_ref, wa_ref, wb_ref, wo_ref, h_ref):
    ya = jnp.dot(oa_ref[0], wa_ref[...], preferred_element_type=F32)
    yb = jnp.dot(ob_ref[0], wb_ref[...], preferred_element_type=F32)
    merged = ga_ref[0] * ya + gb_ref[0] * yb
    h_ref[0] = x_ref[0] + jnp.dot(merged.astype(BF16), wo_ref[...], preferred_element_type=F32)


def _merge(x, oa, ob, ga, gb, wa, wb, wo):
    b, s, d = x.shape
    t = min(MERGE_TILE, s)
    tile = pl.BlockSpec((1, t, d), lambda bi, i: (bi, i, 0))
    return pl.pallas_call(
        _merge_kernel,
        grid=(b, s // t),
        in_specs=[tile, tile, tile, tile, tile,
                  _const_spec(wa.shape), _const_spec(wb.shape), _const_spec(wo.shape)],
        out_specs=tile,
        out_shape=jax.ShapeDtypeStruct((b, s, d), F32),
        compiler_params=pltpu.CompilerParams(
            dimension_semantics=("arbitrary", "arbitrary"), vmem_limit_bytes=VMEM_LIMIT),
        name="merge",
    )(x, oa, ob, ga, gb, wa, wb, wo)


def _ffn_kernel(hp_ref, h_ref, hn_ref, p_ref, gffn_ref, wup_ref, cw_ref, cb_ref, wdn_ref,
                wple_ref, gple_ref, wpg_ref, gfin_ref, y_ref):
    h = h_ref[0]
    t = h.shape[0]
    first_row = pl.program_id(1) * t
    seq_len = pl.num_programs(1) * t
    hh = jnp.concatenate([hp_ref[0], h, hn_ref[0]], axis=0)
    n2 = _rms(hh, gffn_ref[...]).astype(BF16)
    rows = t + 2 * HALO

    r = lax.broadcasted_iota(jnp.int32, (t, 1), 0) + first_row
    keep_prev = r > 0
    keep_next = r < seq_len - 1

    def conv(u, c0):
        w = cw_ref[:, c0:c0 + FFN_CHUNK]
        um = pltpu.roll(u, 1, 0)[HALO:HALO + t]
        up = pltpu.roll(u, rows - 1, 0)[HALO:HALO + t]
        uc = u[HALO:HALO + t]
        um = jnp.where(keep_prev, um, 0.0)
        up = jnp.where(keep_next, up, 0.0)
        return um * w[0:1] + uc * w[1:2] + up * w[2:3] + cb_ref[:, c0:c0 + FFN_CHUNK]

    acc = jnp.zeros((t, D_MODEL), F32)
    for c in range(D_FF // FFN_CHUNK):
        c0 = c * FFN_CHUNK
        ug = jnp.dot(n2, wup_ref[:, c0:c0 + FFN_CHUNK], preferred_element_type=F32)
        uv = jnp.dot(n2, wup_ref[:, D_FF + c0:D_FF + c0 + FFN_CHUNK], preferred_element_type=F32)
        a = jax.nn.gelu(conv(ug, c0)) * conv(uv, D_FF + c0)
        acc = acc + jnp.dot(a.astype(BF16), wdn_ref[c0:c0 + FFN_CHUNK, :], preferred_element_type=F32)
    h2 = h + acc

    n3 = _rms(h2, gple_ref[...]).astype(BF16)
    gate = jax.nn.sigmoid(jnp.dot(n3, wpg_ref[...], preferred_element_type=F32))
    e = jnp.dot(p_ref[0].astype(BF16), wple_ref[...], preferred_element_type=F32)
    h3 = h2 + e * gate
    y_ref[0] = _rms(h3, gfin_ref[...])


def _ffn(h, p, g_ffn, wup, conv_w, conv_b, wdn, wple, g_ple, wpg, g_final):
    b, s, d = h.shape
    t = min(FFN_TILE, s)
    per = t // HALO
    nhb = s // HALO
    tile = pl.BlockSpec((1, t, d), lambda bi, i: (bi, i, 0))
    return pl.pallas_call(
        _ffn_kernel,
        grid=(b, s // t),
        in_specs=[
            pl.BlockSpec((1, HALO, d), lambda bi, i: (bi, jnp.maximum(i * per - 1, 0), 0)),
            tile,
            pl.BlockSpec((1, HALO, d), lambda bi, i: (bi, jnp.minimum((i + 1) * per, nhb - 1), 0)),
            pl.BlockSpec((1, t, PLE_DIM), lambda bi, i: (bi, i, 0)),
            _const_spec((1, d)),
            _const_spec(wup.shape),
            _const_spec(conv_w.shape),
            _const_spec(conv_b.shape),
            _const_spec(wdn.shape),
            _const_spec(wple.shape),
            _const_spec((1, d)),
            _const_spec(wpg.shape),
            _const_spec((1, d)),
        ],
        out_specs=tile,
        out_shape=jax.ShapeDtypeStruct((b, s, d), F32),
        compiler_params=pltpu.CompilerParams(
            dimension_semantics=("arbitrary", "arbitrary"), vmem_limit_bytes=VMEM_LIMIT),
        name="ffn",
    )(h, h, h, p, g_ffn, wup, conv_w, conv_b, wdn, wple, g_ple, wpg, g_final)


def _rope_tables_t(s):
    pos = jnp.arange(s, dtype=jnp.int32)

    def tab(ids, dim, theta):
        inv = theta ** (-jnp.arange(0, dim, 2, dtype=F32) / dim)
        ang = ids.astype(F32)[:, None] * inv[None, :]
        return jnp.cos(ang).T, jnp.sin(ang).T

    ca, sa = tab(pos, A_ROT_DIM, A_ROPE_THETA)
    cr, sr = tab(pos // GRID_W, B_HEAD_DIM // 2, B_AXIAL_THETA)
    cc, sc = tab(pos % GRID_W, B_HEAD_DIM // 2, B_AXIAL_THETA)
    return jnp.concatenate([ca, cr, cc], axis=0), jnp.concatenate([sa, sr, sc], axis=0)


def _encode(x, p, cos_t, sin_t, w):
    qat, ka, vat, qbt, kb, vbt, ga, gb = _project(
        x, cos_t, sin_t, w["g_mix"], w["wt"], w["wg"], w["g_qn"], w["g_kn"])
    oa = _attention_a(w["lq1"], w["lk1"], w["lq2"], w["lk2"], w["g_diff"], qat, ka, vat)
    ob = _attention_b(qbt, kb, vbt)
    h1 = _merge(x, oa, ob, ga, gb, w["w_a"], w["w_b"], w["w_out"])
    return _ffn(h1, p, w["g_ffn"], w["w_up"], w["conv_w"], w["conv_b"], w["w_down"],
                w["w_ple"], w["g_ple"], w["w_ple_gate"], w["g_final"])


def kernel(x_prompt, x_sample, p_prompt, p_sample, g_mix, w_in, lambda_q1, lambda_k1, lambda_q2,
           lambda_k2, g_diff, w_a, g_qn, g_kn, w_b, w_out, g_ffn, w_up, conv_w, conv_b, w_down,
           w_ple, g_ple, w_ple_gate, g_final):
    w_in0 = w_in[0]
    row = lambda v: v.reshape(1, -1).astype(F32)
    col = lambda v: v.reshape(-1, 1).astype(F32)
    w = dict(
        g_mix=row(g_mix[0]),
        wt=w_in0[:, :N_TRANSPOSED].T.astype(BF16),
        wg=w_in0[:, N_TRANSPOSED:].astype(BF16),
        g_qn=col(g_qn[0]), g_kn=col(g_kn[0]),
        lq1=row(lambda_q1[0]), lk1=row(lambda_k1[0]), lq2=row(lambda_q2[0]), lk2=row(lambda_k2[0]),
        g_diff=col(g_diff[0]),
        w_a=w_a[0].astype(BF16), w_b=w_b[0].astype(BF16), w_out=w_out[0].astype(BF16),
        g_ffn=row(g_ffn[0]),
        w_up=w_up[0].astype(BF16), conv_w=conv_w[0].astype(F32), conv_b=row(conv_b[0]),
        w_down=w_down[0].astype(BF16),
        w_ple=w_ple[0].astype(BF16), g_ple=row(g_ple[0]), w_ple_gate=w_ple_gate[0].astype(BF16),
        g_final=row(g_final),
    )
    outs = []
    for x, p in ((x_prompt, p_prompt), (x_sample, p_sample)):
        cos_t, sin_t = _rope_tables_t(x.shape[1])
        outs.append(_encode(x, p[0], cos_t, sin_t, w))
    return tuple(outs)
```

```python
import functools
import math

import jax
import jax.numpy as jnp
from jax import lax
from jax.experimental import pallas as pl
from jax.experimental.pallas import tpu as pltpu

F32 = jnp.float32
BF16 = jnp.bfloat16

D_MODEL = 1024
PLE_DIM = 256
GRID_W = 64
EPS = 1e-6
A_HEADS = 8
A_HEAD_DIM = 64
A_ROT_DIM = A_HEAD_DIM // 4
A_ROPE_THETA = 500000.0
B_HEADS = 16
B_KV_HEADS = 4
B_HEAD_DIM = 64
B_REP = B_HEADS // B_KV_HEADS
B_AXIAL_THETA = 10000.0
D_FF = 2816
LAM_INIT = 0.8 - 0.6 * math.exp(-0.3 * 0)

A_W = A_HEADS * 2 * A_HEAD_DIM
B_Q = B_HEADS * B_HEAD_DIM
B_KV = B_KV_HEADS * B_HEAD_DIM
N_TRANSPOSED = 3 * A_W + B_Q + 2 * B_KV

V7X_VMEM_BYTES = 64 * 1024 * 1024
VMEM_LIMIT = 56 * 1024 * 1024
SUBLANES = 8
LANES = 128

TOKEN_TILE = 512
A_Q_TILE = 512
B_Q_TILE = 256
MERGE_TILE = 512
FFN_TILE = 512
FFN_CHUNK = 256
HALO = SUBLANES
SUM_ROWS = 16
A_VROWS = 2 * A_HEAD_DIM + SUM_ROWS
B_VROWS = B_HEAD_DIM + SUM_ROWS
NEG_BIG = -1e30


def _rms(x, g):
    ms = jnp.mean(x * x, axis=-1, keepdims=True)
    return (x * lax.rsqrt(ms + EPS)) * g


def _rms_rows(x, g):
    ms = jnp.mean(x * x, axis=0, keepdims=True)
    return (x * lax.rsqrt(ms + EPS)) * g


def _proj_kernel(x_ref, cos_ref, sin_ref, gmix_ref, wt_ref, wg_ref, gq_ref, gk_ref,
                 qat_ref, ka_ref, vat_ref, qbt_ref, kb_ref, vbt_ref, ga_ref, gb_ref):
    x = x_ref[0]
    n = _rms(x, gmix_ref[...])
    nb = n.astype(BF16)
    nt = n.T.astype(BF16)

    cos = cos_ref[...]
    sin = sin_ref[...]
    ca, sa = cos[0:8], sin[0:8]
    cr, sr = cos[8:24], sin[8:24]
    cc, sc = cos[24:40], sin[24:40]

    def rope_a(p):
        x1, x2 = p[0:8], p[8:16]
        return jnp.concatenate([x1 * ca - x2 * sa, x2 * ca + x1 * sa, p[16:64]], axis=0)

    def rope_b(p):
        a1, a2, b1, b2 = p[0:16], p[16:32], p[32:48], p[48:64]
        return jnp.concatenate([a1 * cr - a2 * sr, a2 * cr + a1 * sr,
                                b1 * cc - b2 * sc, b2 * cc + b1 * sc], axis=0)

    def proj_t(lo, hi):
        return jnp.dot(wt_ref[lo:hi, :], nt, preferred_element_type=F32)

    scale = A_HEAD_DIM ** -0.5 * math.log2(math.e)

    ones_rows = (lax.broadcasted_iota(jnp.int32, (SUM_ROWS, nt.shape[1]), 0) == 0).astype(BF16)

    def put_aq(p):
        for g in range(A_W // 64):
            qat_ref[0, g * 64:(g + 1) * 64, :] = (rope_a(p[g * 64:(g + 1) * 64]) * scale).astype(BF16)

    def put_ak(p):
        for s in range(A_W // LANES):
            blk = jnp.concatenate([rope_a(p[s * 128:s * 128 + 64]),
                                   rope_a(p[s * 128 + 64:(s + 1) * 128])], axis=0)
            ka_ref[0, :, s * LANES:(s + 1) * LANES] = blk.T.astype(BF16)

    def put_v(p, v_ref, heads, dv):
        vrows = dv + SUM_ROWS
        for h in range(heads):
            v_ref[0, 0, h * vrows:h * vrows + dv, :] = p[h * dv:(h + 1) * dv].astype(BF16)
            v_ref[0, 0, h * vrows + dv:(h + 1) * vrows, :] = ones_rows

    def put_bq(p):
        gq = gq_ref[...]
        for h in range(B_HEADS):
            y = rope_b(_rms_rows(p[h * 64:(h + 1) * 64], gq))
            qbt_ref[0, h * 64:(h + 1) * 64, :] = (y * scale).astype(BF16)

    def put_bk(p):
        gk = gk_ref[...]
        zeros = jnp.zeros((LANES - B_HEAD_DIM, p.shape[1]), F32)
        for g in range(B_KV_HEADS):
            y = rope_b(_rms_rows(p[g * 64:(g + 1) * 64], gk))
            kb_ref[0, g] = jnp.concatenate([y, zeros], axis=0).T.astype(BF16)

    off_bq = 3 * A_W
    off_bk = off_bq + B_Q
    p_aq = proj_t(0, A_W)
    p_ak = proj_t(A_W, 2 * A_W)
    put_aq(p_aq)
    p_av = proj_t(2 * A_W, off_bq)
    put_ak(p_ak)
    p_bq = proj_t(off_bq, off_bk)
    put_v(p_av, vat_ref, A_HEADS, 2 * A_HEAD_DIM)
    p_bkv = proj_t(off_bk, off_bk + 2 * B_KV)
    put_bq(p_bq)
    gates = jnp.dot(nb, wg_ref[...], preferred_element_type=F32)
    put_bk(p_bkv[:B_KV])
    put_v(p_bkv[B_KV:], vbt_ref, B_KV_HEADS, B_HEAD_DIM)
    ga_ref[0] = jax.nn.sigmoid(gates[:, :D_MODEL])
    gb_ref[0] = jax.nn.sigmoid(gates[:, D_MODEL:])


def _const_spec(shape):
    nd = len(shape)
    return pl.BlockSpec(shape, lambda *_: (0,) * nd, pipeline_mode=pl.Buffered(1))


def _project(x, cos_t, sin_t, g_mix, wt, wg, gq, gk):
    b, s, d = x.shape
    t = min(TOKEN_TILE, s)
    nt = s // t
    out_shape = (
        jax.ShapeDtypeStruct((b, A_W, s), BF16),
        jax.ShapeDtypeStruct((b, s, A_W), BF16),
        jax.ShapeDtypeStruct((b, nt, A_HEADS * A_VROWS, t), BF16),
        jax.ShapeDtypeStruct((b, B_Q, s), BF16),
        jax.ShapeDtypeStruct((b, B_KV_HEADS, s, LANES), BF16),
        jax.ShapeDtypeStruct((b, nt, B_KV_HEADS * B_VROWS, t), BF16),
        jax.ShapeDtypeStruct((b, s, D_MODEL), F32),
        jax.ShapeDtypeStruct((b, s, D_MODEL), F32),
    )
    in_specs = [
        pl.BlockSpec((1, t, d), lambda bi, i: (bi, i, 0)),
        pl.BlockSpec((40, t), lambda bi, i: (0, i)),
        pl.BlockSpec((40, t), lambda bi, i: (0, i)),
        _const_spec((1, d)),
        _const_spec(wt.shape),
        _const_spec(wg.shape),
        _const_spec((B_HEAD_DIM, 1)),
        _const_spec((B_HEAD_DIM, 1)),
    ]
    out_specs = (
        pl.BlockSpec((1, A_W, t), lambda bi, i: (bi, 0, i)),
        pl.BlockSpec((1, t, A_W), lambda bi, i: (bi, i, 0)),
        pl.BlockSpec((1, 1, A_HEADS * A_VROWS, t), lambda bi, i: (bi, i, 0, 0)),
        pl.BlockSpec((1, B_Q, t), lambda bi, i: (bi, 0, i)),
        pl.BlockSpec((1, B_KV_HEADS, t, LANES), lambda bi, i: (bi, 0, i, 0)),
        pl.BlockSpec((1, 1, B_KV_HEADS * B_VROWS, t), lambda bi, i: (bi, i, 0, 0)),
        pl.BlockSpec((1, t, D_MODEL), lambda bi, i: (bi, i, 0)),
        pl.BlockSpec((1, t, D_MODEL), lambda bi, i: (bi, i, 0)),
    )
    return pl.pallas_call(
        _proj_kernel,
        grid=(b, nt),
        in_specs=in_specs,
        out_specs=out_specs,
        out_shape=out_shape,
        compiler_params=pltpu.CompilerParams(
            dimension_semantics=("arbitrary", "arbitrary"), vmem_limit_bytes=VMEM_LIMIT),
        name="proj",
    )(x, cos_t, sin_t, g_mix, wt, wg, gq, gk)


KEY_BLOCK = 256


def _flash_t(q_t, k_rows, v_tile_t, s_refs, acc_ref, n_tiles):
    s_a, s_b = s_refs
    tk = s_a.shape[0]
    acc_ref[...] = jnp.zeros_like(acc_ref)

    def scores(j, s_ref):
        s = jnp.dot(k_rows(pl.multiple_of(j * tk, tk), tk), q_t, preferred_element_type=F32)
        s_ref[...] = s
        return jnp.max(s, axis=0, keepdims=True)

    def consume(v_t, s_ref, m, bmax):
        m_new = jnp.maximum(m, bmax)
        alpha = jnp.exp2(m - m_new)
        p = jnp.exp2(s_ref[...] - m_new).astype(BF16)
        acc_ref[...] = alpha * acc_ref[...] + jnp.dot(v_t, p, preferred_element_type=F32)
        return m_new

    def tile(i, carry, last):
        m, bmax_a = carry
        v = v_tile_t(i)
        assert v.shape[1] == 2 * tk
        bmax_b = scores(2 * i + 1, s_b)
        m = consume(v[:, :tk], s_a, m, bmax_a)
        if not last:
            bmax_a = scores(2 * i + 2, s_a)
        m = consume(v[:, tk:], s_b, m, bmax_b)
        return m, bmax_a

    bmax = scores(0, s_a)
    carry = (jnp.full_like(bmax, NEG_BIG), bmax)
    n_loop = n_tiles - 1
    unroll = next(u for u in (5, 3, 1) if n_loop % u == 0)
    carry = lax.fori_loop(0, n_loop, lambda i, c: tile(i, c, False), carry, unroll=unroll)
    tile(n_tiles - 1, carry, True)


def _attn_a_kernel(lq1_ref, lk1_ref, lq2_ref, lk2_ref, gd_ref, qt_ref, k_ref, vt_ref,
                   o_ref, s_a, s_b, acc_ref):
    q = qt_ref[0]
    tq = q.shape[1]
    row = lax.broadcasted_iota(jnp.int32, q.shape, 0)
    zero = jnp.zeros_like(q)
    q_t = jnp.concatenate([jnp.where(row < A_HEAD_DIM, q, zero),
                           jnp.where(row >= A_HEAD_DIM, q, zero)], axis=1)
    _flash_t(q_t, lambda r0, n: k_ref[0, pl.ds(r0, n), :], lambda i: vt_ref[0, i],
             (s_a, s_b), acc_ref, vt_ref.shape[1])

    lam = (jnp.exp(jnp.sum(lq1_ref[...] * lk1_ref[...], axis=-1, keepdims=True))
           - jnp.exp(jnp.sum(lq2_ref[...] * lk2_ref[...], axis=-1, keepdims=True))
           + LAM_INIT)
    dv = 2 * A_HEAD_DIM
    acc = acc_ref[0:dv, :]
    l = acc_ref[dv:dv + 1, :]
    o = acc[:, :tq] / l[:, :tq] - lam * (acc[:, tq:] / l[:, tq:])
    y = _rms_rows(o, gd_ref[...]) * (1.0 - LAM_INIT)
    o_ref[0] = y.T.astype(BF16)


def _attn_b_kernel(qt_ref, k_ref, vt_ref, o_ref, s_a, s_b, acc_ref):
    q = qt_ref[0]
    tq = q.shape[1]
    q_cat = jnp.concatenate([q[r * 64:(r + 1) * 64] for r in range(B_REP)], axis=1)
    q_t = jnp.concatenate([q_cat, jnp.zeros_like(q_cat)], axis=0)
    _flash_t(q_t, lambda r0, n: k_ref[0, 0, pl.ds(r0, n), :], lambda i: vt_ref[0, i],
             (s_a, s_b), acc_ref, vt_ref.shape[1])

    o = acc_ref[0:B_HEAD_DIM, :] / acc_ref[B_HEAD_DIM:B_HEAD_DIM + 1, :]
    o_rows = jnp.concatenate([o[:, r * tq:(r + 1) * tq] for r in range(B_REP)], axis=0)
    o_ref[0] = o_rows.T.astype(BF16)


def _attention_a(lq1, lk1, lq2, lk2, g_diff, qat, ka, vat):
    b, _, s = qat.shape
    tk = vat.shape[3]
    tq = min(A_Q_TILE, s)
    small = lambda shape: pl.BlockSpec(shape, lambda bi, h, i: (0,) * len(shape))
    return pl.pallas_call(
        _attn_a_kernel,
        grid=(b, A_HEADS, s // tq),
        in_specs=[
            small((1, A_HEAD_DIM)), small((1, A_HEAD_DIM)), small((1, A_HEAD_DIM)), small((1, A_HEAD_DIM)),
            small((2 * A_HEAD_DIM, 1)),
            pl.BlockSpec((1, 2 * A_HEAD_DIM, tq), lambda bi, h, i: (bi, h, i)),
            pl.BlockSpec((1, s, 2 * A_HEAD_DIM), lambda bi, h, i: (bi, 0, h)),
            pl.BlockSpec((1, s // tk, A_VROWS, tk), lambda bi, h, i: (bi, 0, h, 0)),
        ],
        out_specs=pl.BlockSpec((1, tq, 2 * A_HEAD_DIM), lambda bi, h, i: (bi, i, h)),
        out_shape=jax.ShapeDtypeStruct((b, s, A_W), BF16),
        scratch_shapes=[pltpu.VMEM((KEY_BLOCK, 2 * tq), F32), pltpu.VMEM((KEY_BLOCK, 2 * tq), F32),
                        pltpu.VMEM((A_VROWS, 2 * tq), F32)],
        compiler_params=pltpu.CompilerParams(
            dimension_semantics=("arbitrary", "arbitrary", "arbitrary"), vmem_limit_bytes=VMEM_LIMIT),
        name="attn_a",
    )(lq1, lk1, lq2, lk2, g_diff, qat, ka, vat)


def _attention_b(qbt, kb, vbt):
    b, _, s = qbt.shape
    tk = vbt.shape[3]
    tq = min(B_Q_TILE, s)
    rows = B_REP * B_HEAD_DIM
    return pl.pallas_call(
        _attn_b_kernel,
        grid=(b, B_KV_HEADS, s // tq),
        in_specs=[
            pl.BlockSpec((1, rows, tq), lambda bi, g, i: (bi, g, i)),
            pl.BlockSpec((1, 1, s, LANES), lambda bi, g, i: (bi, g, 0, 0)),
            pl.BlockSpec((1, s // tk, B_VROWS, tk), lambda bi, g, i: (bi, 0, g, 0)),
        ],
        out_specs=pl.BlockSpec((1, tq, rows), lambda bi, g, i: (bi, i, g)),
        out_shape=jax.ShapeDtypeStruct((b, s, B_Q), BF16),
        scratch_shapes=[pltpu.VMEM((KEY_BLOCK, B_REP * tq), F32), pltpu.VMEM((KEY_BLOCK, B_REP * tq), F32),
                        pltpu.VMEM((B_VROWS, B_REP * tq), F32)],
        compiler_params=pltpu.CompilerParams(
            dimension_semantics=("arbitrary", "arbitrary", "arbitrary"), vmem_limit_bytes=VMEM_LIMIT),
        name="attn_b",
    )(qbt, kb, vbt)


def _merge_kernel(x_ref, oa_ref, ob_ref, ga_ref, gb_ref, wa_ref, wb_ref, wo_ref, h_ref):
    ya = jnp.dot(oa_ref[0], wa_ref[...], preferred_element_type=F32)
    yb = jnp.dot(ob_ref[0], wb_ref[...], preferred_element_type=F32)
    merged = ga_ref[0] * ya + gb_ref[0] * yb
    h_ref[0] = x_ref[0] + jnp.dot(merged.astype(BF16), wo_ref[...], preferred_element_type=F32)


def _merge(x, oa, ob, ga, gb, wa, wb, wo):
    b, s, d = x.shape
    t = min(MERGE_TILE, s)
    tile = pl.BlockSpec((1, t, d), lambda bi, i: (bi, i, 0))
    return pl.pallas_call(
        _merge_kernel,
        grid=(b, s // t),
        in_specs=[tile, tile, tile, tile, tile,
                  _const_spec(wa.shape), _const_spec(wb.shape), _const_spec(wo.shape)],
        out_specs=tile,
        out_shape=jax.ShapeDtypeStruct((b, s, d), F32),
        compiler_params=pltpu.CompilerParams(
            dimension_semantics=("arbitrary", "arbitrary"), vmem_limit_bytes=VMEM_LIMIT),
        name="merge",
    )(x, oa, ob, ga, gb, wa, wb, wo)


def _ffn_kernel(hp_ref, h_ref, hn_ref, p_ref, gffn_ref, wup_ref, cw_ref, cb_ref, wdn_ref,
                wple_ref, gple_ref, wpg_ref, gfin_ref, y_ref, ug_ref, uv_ref, a_ref):
    h = h_ref[0]
    t = h.shape[0]
    rows = t + 2 * HALO
    first_row = pl.program_id(1) * t
    seq_len = pl.num_programs(1) * t
    hh = jnp.concatenate([hp_ref[0], h, hn_ref[0]], axis=0)
    r = lax.broadcasted_iota(jnp.int32, (rows, 1), 0) + (first_row - HALO)
    inside = jnp.logical_and(r >= 0, r < seq_len)
    n2 = jnp.where(inside, _rms(hh, gffn_ref[...]), 0.0).astype(BF16)

    def up_proj(c):
        c0 = c * FFN_CHUNK
        ug_ref[c % 2] = jnp.dot(n2, wup_ref[:, c0:c0 + FFN_CHUNK], preferred_element_type=F32)
        uv_ref[c % 2] = jnp.dot(n2, wup_ref[:, D_FF + c0:D_FF + c0 + FFN_CHUNK],
                                preferred_element_type=F32)

    def conv(u_ref, slot, c0):
        w = cw_ref[:, c0:c0 + FFN_CHUNK]
        um = u_ref[slot, HALO - 1:HALO - 1 + t, :]
        uc = u_ref[slot, HALO:HALO + t, :]
        up = u_ref[slot, HALO + 1:HALO + 1 + t, :]
        return um * w[0:1] + uc * w[1:2] + up * w[2:3] + cb_ref[:, c0:c0 + FFN_CHUNK]

    n_chunks = D_FF // FFN_CHUNK
    split = (n_chunks + 1) // 2 * FFN_CHUNK
    up_proj(0)
    for c in range(n_chunks):
        if c + 1 < n_chunks:
            up_proj(c + 1)
        c0 = c * FFN_CHUNK
        a = jax.nn.gelu(conv(ug_ref, c % 2, c0)) * conv(uv_ref, c % 2, D_FF + c0)
        a_ref[:, c0:c0 + FFN_CHUNK] = a.astype(BF16)
        if c0 + FFN_CHUNK == split:
            down = jnp.dot(a_ref[:, :split], wdn_ref[:split, :], preferred_element_type=F32)
    down = down + jnp.dot(a_ref[:, split:], wdn_ref[split:, :], preferred_element_type=F32)
    h2 = h + down

    n3 = _rms(h2, gple_ref[...]).astype(BF16)
    gate = jax.nn.sigmoid(jnp.dot(n3, wpg_ref[...], preferred_element_type=F32))
    e = jnp.dot(p_ref[0].astype(BF16), wple_ref[...], preferred_element_type=F32)
    h3 = h2 + e * gate
    y_ref[0] = _rms(h3, gfin_ref[...])


def _ffn(h, p, g_ffn, wup, conv_w, conv_b, wdn, wple, g_ple, wpg, g_final):
    b, s, d = h.shape
    t = min(FFN_TILE, s)
    per = t // HALO
    nhb = s // HALO
    tile = pl.BlockSpec((1, t, d), lambda bi, i: (bi, i, 0))
    return pl.pallas_call(
        _ffn_kernel,
        grid=(b, s // t),
        in_specs=[
            pl.BlockSpec((1, HALO, d), lambda bi, i: (bi, jnp.maximum(i * per - 1, 0), 0)),
            tile,
            pl.BlockSpec((1, HALO, d), lambda bi, i: (bi, jnp.minimum((i + 1) * per, nhb - 1), 0)),
            pl.BlockSpec((1, t, PLE_DIM), lambda bi, i: (bi, i, 0)),
            _const_spec((1, d)),
            _const_spec(wup.shape),
            _const_spec(conv_w.shape),
            _const_spec(conv_b.shape),
            _const_spec(wdn.shape),
            _const_spec(wple.shape),
            _const_spec((1, d)),
            _const_spec(wpg.shape),
            _const_spec((1, d)),
        ],
        out_specs=tile,
        out_shape=jax.ShapeDtypeStruct((b, s, d), F32),
        scratch_shapes=[pltpu.VMEM((2, t + 2 * HALO, FFN_CHUNK), F32),
                        pltpu.VMEM((2, t + 2 * HALO, FFN_CHUNK), F32),
                        pltpu.VMEM((t, D_FF), BF16)],
        compiler_params=pltpu.CompilerParams(
            dimension_semantics=("arbitrary", "arbitrary"), vmem_limit_bytes=VMEM_LIMIT),
        name="ffn",
    )(h, h, h, p, g_ffn, wup, conv_w, conv_b, wdn, wple, g_ple, wpg, g_final)


def _rope_tables_t(s):
    pos = jnp.arange(s, dtype=jnp.int32)

    def tab(ids, dim, theta):
        inv = theta ** (-jnp.arange(0, dim, 2, dtype=F32) / dim)
        ang = ids.astype(F32)[:, None] * inv[None, :]
        return jnp.cos(ang).T, jnp.sin(ang).T

    ca, sa = tab(pos, A_ROT_DIM, A_ROPE_THETA)
    cr, sr = tab(pos // GRID_W, B_HEAD_DIM // 2, B_AXIAL_THETA)
    cc, sc = tab(pos % GRID_W, B_HEAD_DIM // 2, B_AXIAL_THETA)
    return jnp.concatenate([ca, cr, cc], axis=0), jnp.concatenate([sa, sr, sc], axis=0)


def _encode(x, p, cos_t, sin_t, w):
    qat, ka, vat, qbt, kb, vbt, ga, gb = _project(
        x, cos_t, sin_t, w["g_mix"], w["wt"], w["wg"], w["g_qn"], w["g_kn"])
    oa = _attention_a(w["lq1"], w["lk1"], w["lq2"], w["lk2"], w["g_diff"], qat, ka, vat)
    ob = _attention_b(qbt, kb, vbt)
    h1 = _merge(x, oa, ob, ga, gb, w["w_a"], w["w_b"], w["w_out"])
    return _ffn(h1, p, w["g_ffn"], w["w_up"], w["conv_w"], w["conv_b"], w["w_down"],
                w["w_ple"], w["g_ple"], w["w_ple_gate"], w["g_final"])


def kernel(x_prompt, x_sample, p_prompt, p_sample, g_mix, w_in, lambda_q1, lambda_k1, lambda_q2,
           lambda_k2, g_diff, w_a, g_qn, g_kn, w_b, w_out, g_ffn, w_up, conv_w, conv_b, w_down,
           w_ple, g_ple, w_ple_gate, g_final):
    w_in0 = w_in[0]
    row = lambda v: v.reshape(1, -1).astype(F32)
    col = lambda v: v.reshape(-1, 1).astype(F32)
    w = dict(
        g_mix=row(g_mix[0]),
        wt=w_in0[:, :N_TRANSPOSED].T.astype(BF16),
        wg=w_in0[:, N_TRANSPOSED:].astype(BF16),
        g_qn=col(g_qn[0]), g_kn=col(g_kn[0]),
        lq1=row(lambda_q1[0]), lk1=row(lambda_k1[0]), lq2=row(lambda_q2[0]), lk2=row(lambda_k2[0]),
        g_diff=col(g_diff[0]),
        w_a=w_a[0].astype(BF16), w_b=w_b[0].astype(BF16), w_out=w_out[0].astype(BF16),
        g_ffn=row(g_ffn[0]),
        w_up=w_up[0].astype(BF16), conv_w=conv_w[0].astype(F32), conv_b=row(conv_b[0]),
        w_down=w_down[0].astype(BF16),
        w_ple=w_ple[0].astype(BF16), g_ple=row(g_ple[0]), w_ple_gate=w_ple_gate[0].astype(BF16),
        g_final=row(g_final),
    )
    outs = []
    for x, p in ((x_prompt, p_prompt), (x_sample, p_sample)):
        cos_t, sin_t = _rope_tables_t(x.shape[1])
        outs.append(_encode(x, p[0], cos_t, sin_t, w))
    return tuple(outs)
```

```python
import functools
import math

import jax
import jax.numpy as jnp
from jax import lax
from jax.experimental import pallas as pl
from jax.experimental.pallas import tpu as pltpu

F32 = jnp.float32
BF16 = jnp.bfloat16

D_MODEL = 1024
PLE_DIM = 256
GRID_W = 64
EPS = 1e-6
A_HEADS = 8
A_HEAD_DIM = 64
A_ROT_DIM = A_HEAD_DIM // 4
A_ROPE_THETA = 500000.0
B_HEADS = 16
B_KV_HEADS = 4
B_HEAD_DIM = 64
B_REP = B_HEADS // B_KV_HEADS
B_AXIAL_THETA = 10000.0
D_FF = 2816
LAM_INIT = 0.8 - 0.6 * math.exp(-0.3 * 0)

A_W = A_HEADS * 2 * A_HEAD_DIM
B_Q = B_HEADS * B_HEAD_DIM
B_KV = B_KV_HEADS * B_HEAD_DIM
N_TRANSPOSED = 3 * A_W + B_Q + 2 * B_KV

V7X_VMEM_BYTES = 64 * 1024 * 1024
VMEM_LIMIT = 56 * 1024 * 1024
SUBLANES = 8
LANES = 128

TOKEN_TILE = 512
A_Q_TILE = 512
B_Q_TILE = 256
MERGE_TILE = 512
FFN_TILE = 512
FFN_CHUNK = 256
HALO = SUBLANES
SUM_ROWS = 16
A_VROWS = 2 * A_HEAD_DIM + SUM_ROWS
B_VROWS = B_HEAD_DIM + SUM_ROWS
NEG_BIG = -1e30


def _rms(x, g):
    ms = jnp.mean(x * x, axis=-1, keepdims=True)
    return (x * lax.rsqrt(ms + EPS)) * g


def _rms_rows(x, g):
    ms = jnp.mean(x * x, axis=0, keepdims=True)
    return (x * lax.rsqrt(ms + EPS)) * g


def _proj_kernel(x_ref, cos_ref, sin_ref, gmix_ref, wt_ref, wg_ref, gq_ref, gk_ref,
                 qat_ref, ka_ref, vat_ref, qbt_ref, kb_ref, vbt_ref, ga_ref, gb_ref, kna_ref, knb_ref):
    x = x_ref[0]
    n = _rms(x, gmix_ref[...])
    nb = n.astype(BF16)
    nt = n.T.astype(BF16)

    cos = cos_ref[...]
    sin = sin_ref[...]
    ca, sa = cos[0:8], sin[0:8]
    cr, sr = cos[8:24], sin[8:24]
    cc, sc = cos[24:40], sin[24:40]

    def rope_a(p):
        x1, x2 = p[0:8], p[8:16]
        return jnp.concatenate([x1 * ca - x2 * sa, x2 * ca + x1 * sa, p[16:64]], axis=0)

    def rope_b(p):
        a1, a2, b1, b2 = p[0:16], p[16:32], p[32:48], p[48:64]
        return jnp.concatenate([a1 * cr - a2 * sr, a2 * cr + a1 * sr,
                                b1 * cc - b2 * sc, b2 * cc + b1 * sc], axis=0)

    def proj_t(lo, hi):
        return jnp.dot(wt_ref[lo:hi, :], nt, preferred_element_type=F32)

    def max_sq_norm(k_t):
        sq = jnp.sum(k_t * k_t, axis=0, keepdims=True)
        return jnp.full((SUBLANES, LANES), jnp.max(sq), F32)

    scale = A_HEAD_DIM ** -0.5 * math.log2(math.e)

    ones_rows = (lax.broadcasted_iota(jnp.int32, (SUM_ROWS, nt.shape[1]), 0) == 0).astype(BF16)

    def put_aq(p):
        for g in range(A_W // 64):
            qat_ref[0, g * 64:(g + 1) * 64, :] = (rope_a(p[g * 64:(g + 1) * 64]) * scale).astype(BF16)

    def put_ak(p):
        for s in range(A_W // LANES):
            blk = jnp.concatenate([rope_a(p[s * 128:s * 128 + 64]),
                                   rope_a(p[s * 128 + 64:(s + 1) * 128])], axis=0)
            ka_ref[0, :, s * LANES:(s + 1) * LANES] = blk.T.astype(BF16)
            kna_ref[0, s] = max_sq_norm(blk)

    def put_v(p, v_ref, heads, dv):
        vrows = dv + SUM_ROWS
        for h in range(heads):
            v_ref[0, 0, h * vrows:h * vrows + dv, :] = p[h * dv:(h + 1) * dv].astype(BF16)
            v_ref[0, 0, h * vrows + dv:(h + 1) * vrows, :] = ones_rows

    def put_bq(p):
        gq = gq_ref[...]
        for h in range(B_HEADS):
            y = rope_b(_rms_rows(p[h * 64:(h + 1) * 64], gq))
            qbt_ref[0, h * 64:(h + 1) * 64, :] = (y * scale).astype(BF16)

    def put_bk(p):
        gk = gk_ref[...]
        zeros = jnp.zeros((LANES - B_HEAD_DIM, p.shape[1]), F32)
        for g in range(B_KV_HEADS):
            y = rope_b(_rms_rows(p[g * 64:(g + 1) * 64], gk))
            kb_ref[0, g] = jnp.concatenate([y, zeros], axis=0).T.astype(BF16)
            knb_ref[0, g] = max_sq_norm(y)

    off_bq = 3 * A_W
    off_bk = off_bq + B_Q
    p_aq = proj_t(0, A_W)
    p_ak = proj_t(A_W, 2 * A_W)
    put_aq(p_aq)
    p_av = proj_t(2 * A_W, off_bq)
    put_ak(p_ak)
    p_bq = proj_t(off_bq, off_bk)
    put_v(p_av, vat_ref, A_HEADS, 2 * A_HEAD_DIM)
    p_bkv = proj_t(off_bk, off_bk + 2 * B_KV)
    put_bq(p_bq)
    gates = jnp.dot(nb, wg_ref[...], preferred_element_type=F32)
    put_bk(p_bkv[:B_KV])
    put_v(p_bkv[B_KV:], vbt_ref, B_KV_HEADS, B_HEAD_DIM)
    ga_ref[0] = jax.nn.sigmoid(gates[:, :D_MODEL])
    gb_ref[0] = jax.nn.sigmoid(gates[:, D_MODEL:])


def _const_spec(shape):
    nd = len(shape)
    return pl.BlockSpec(shape, lambda *_: (0,) * nd, pipeline_mode=pl.Buffered(1))


def _project(x, cos_t, sin_t, g_mix, wt, wg, gq, gk):
    b, s, d = x.shape
    t = min(TOKEN_TILE, s)
    nt = s // t
    out_shape = (
        jax.ShapeDtypeStruct((b, A_W, s), BF16),
        jax.ShapeDtypeStruct((b, s, A_W), BF16),
        jax.ShapeDtypeStruct((b, nt, A_HEADS * A_VROWS, t), BF16),
        jax.ShapeDtypeStruct((b, B_Q, s), BF16),
        jax.ShapeDtypeStruct((b, B_KV_HEADS, s, LANES), BF16),
        jax.ShapeDtypeStruct((b, nt, B_KV_HEADS * B_VROWS, t), BF16),
        jax.ShapeDtypeStruct((b, s, D_MODEL), F32),
        jax.ShapeDtypeStruct((b, s, D_MODEL), F32),
        jax.ShapeDtypeStruct((b, A_HEADS, nt * SUBLANES, LANES), F32),
        jax.ShapeDtypeStruct((b, B_KV_HEADS, nt * SUBLANES, LANES), F32),
    )
    in_specs = [
        pl.BlockSpec((1, t, d), lambda bi, i: (bi, i, 0)),
        pl.BlockSpec((40, t), lambda bi, i: (0, i)),
        pl.BlockSpec((40, t), lambda bi, i: (0, i)),
        _const_spec((1, d)),
        _const_spec(wt.shape),
        _const_spec(wg.shape),
        _const_spec((B_HEAD_DIM, 1)),
        _const_spec((B_HEAD_DIM, 1)),
    ]
    out_specs = (
        pl.BlockSpec((1, A_W, t), lambda bi, i: (bi, 0, i)),
        pl.BlockSpec((1, t, A_W), lambda bi, i: (bi, i, 0)),
        pl.BlockSpec((1, 1, A_HEADS * A_VROWS, t), lambda bi, i: (bi, i, 0, 0)),
        pl.BlockSpec((1, B_Q, t), lambda bi, i: (bi, 0, i)),
        pl.BlockSpec((1, B_KV_HEADS, t, LANES), lambda bi, i: (bi, 0, i, 0)),
        pl.BlockSpec((1, 1, B_KV_HEADS * B_VROWS, t), lambda bi, i: (bi, i, 0, 0)),
        pl.BlockSpec((1, t, D_MODEL), lambda bi, i: (bi, i, 0)),
        pl.BlockSpec((1, t, D_MODEL), lambda bi, i: (bi, i, 0)),
        pl.BlockSpec((1, A_HEADS, SUBLANES, LANES), lambda bi, i: (bi, 0, i, 0)),
        pl.BlockSpec((1, B_KV_HEADS, SUBLANES, LANES), lambda bi, i: (bi, 0, i, 0)),
    )
    return pl.pallas_call(
        _proj_kernel,
        grid=(b, nt),
        in_specs=in_specs,
        out_specs=out_specs,
        out_shape=out_shape,
        compiler_params=pltpu.CompilerParams(
            dimension_semantics=("arbitrary", "arbitrary"), vmem_limit_bytes=VMEM_LIMIT),
        name="proj",
    )(x, cos_t, sin_t, g_mix, wt, wg, gq, gk)


KEY_BLOCK = 256


def _flash_t(q_t, k_rows, v_tile_t, s_refs, acc_ref, n_tiles, track_max):
    s_a, s_b = s_refs
    tk = s_a.shape[0]
    acc_ref[...] = jnp.zeros_like(acc_ref)

    def scores(j, s_ref):
        s = jnp.dot(k_rows(pl.multiple_of(j * tk, tk), tk), q_t, preferred_element_type=F32)
        s_ref[...] = s
        return jnp.max(s, axis=0, keepdims=True) if track_max else None

    def consume(v_t, s_ref, m, bmax):
        if not track_max:
            p = jnp.exp2(s_ref[...]).astype(BF16)
            acc_ref[...] += jnp.dot(v_t, p, preferred_element_type=F32)
            return m
        m_new = jnp.maximum(m, bmax)
        alpha = jnp.exp2(m - m_new)
        p = jnp.exp2(s_ref[...] - m_new).astype(BF16)
        acc_ref[...] = alpha * acc_ref[...] + jnp.dot(v_t, p, preferred_element_type=F32)
        return m_new

    def tile(i, carry, last):
        m, bmax_a = carry
        v = v_tile_t(i)
        assert v.shape[1] == 2 * tk
        bmax_b = scores(2 * i + 1, s_b)
        m = consume(v[:, :tk], s_a, m, bmax_a)
        if not last:
            bmax_a = scores(2 * i + 2, s_a)
        m = consume(v[:, tk:], s_b, m, bmax_b)
        return m, bmax_a

    bmax = scores(0, s_a)
    carry = (jnp.full((1, q_t.shape[1]), NEG_BIG, F32), bmax) if track_max else (None, None)
    n_loop = n_tiles - 1
    unroll = next(u for u in (5, 3, 1) if n_loop % u == 0)
    carry = lax.fori_loop(0, n_loop, lambda i, c: tile(i, c, False), carry, unroll=unroll)
    tile(n_tiles - 1, carry, True)


SCORE_BOUND = 64.0
BOUND_SLACK = 1.05


def _attend(q_t, k_rows, v_tile_t, kn_ref, s_refs, acc_ref, n_tiles):
    q32 = q_t.astype(F32)
    q_sq = jnp.max(jnp.sum(q32 * q32, axis=0, keepdims=True))
    k_sq = jnp.max(kn_ref[0, 0])
    bounded = q_sq * k_sq * (BOUND_SLACK * BOUND_SLACK) <= SCORE_BOUND * SCORE_BOUND

    @pl.when(bounded)
    def _():
        _flash_t(q_t, k_rows, v_tile_t, s_refs, acc_ref, n_tiles, track_max=False)

    @pl.when(jnp.logical_not(bounded))
    def _():
        _flash_t(q_t, k_rows, v_tile_t, s_refs, acc_ref, n_tiles, track_max=True)


def _attn_a_kernel(lq1_ref, lk1_ref, lq2_ref, lk2_ref, gd_ref, qt_ref, k_ref, vt_ref, kn_ref,
                   o_ref, s_a, s_b, acc_ref):
    q = qt_ref[0]
    tq = q.shape[1]
    row = lax.broadcasted_iota(jnp.int32, q.shape, 0)
    zero = jnp.zeros_like(q)
    q_t = jnp.concatenate([jnp.where(row < A_HEAD_DIM, q, zero),
                           jnp.where(row >= A_HEAD_DIM, q, zero)], axis=1)
    _attend(q_t, lambda r0, n: k_ref[0, pl.ds(r0, n), :], lambda i: vt_ref[0, i], kn_ref,
            (s_a, s_b), acc_ref, vt_ref.shape[1])

    lam =(jnp.exp(jnp.sum(lq1_ref[...] * lk1_ref[...], axis=-1, keepdims=True))
           - jnp.exp(jnp.sum(lq2_ref[...] * lk2_ref[...], axis=-1, keepdims=True))
           + LAM_INIT)
    dv = 2 * A_HEAD_DIM
    acc = acc_ref[0:dv, :]
    l = acc_ref[dv:dv + 1, :]
    o = acc[:, :tq] / l[:, :tq] - lam * (acc[:, tq:] / l[:, tq:])
    y = _rms_rows(o, gd_ref[...]) * (1.0 - LAM_INIT)
    o_ref[0] = y.T.astype(BF16)


def _attn_b_kernel(qt_ref, k_ref, vt_ref, kn_ref, o_ref, s_a, s_b, acc_ref):
    q = qt_ref[0]
    tq = q.shape[1]
    q_cat = jnp.concatenate([q[r * 64:(r + 1) * 64] for r in range(B_REP)], axis=1)
    q_t = jnp.concatenate([q_cat, jnp.zeros_like(q_cat)], axis=0)
    _attend(q_t, lambda r0, n: k_ref[0, 0, pl.ds(r0, n), :], lambda i: vt_ref[0, i], kn_ref,
            (s_a, s_b), acc_ref, vt_ref.shape[1])

    o = acc_ref[0:B_HEAD_DIM, :] / acc_ref[B_HEAD_DIM:B_HEAD_DIM + 1, :]
    o_rows = jnp.concatenate([o[:, r * tq:(r + 1) * tq] for r in range(B_REP)], axis=0)
    o_ref[0] = o_rows.T.astype(BF16)


def _attention_a(lq1, lk1, lq2, lk2, g_diff, qat, ka, vat, kn):
    b, _, s = qat.shape
    tk = vat.shape[3]
    tq = min(A_Q_TILE, s)
    small = lambda shape: pl.BlockSpec(shape, lambda bi, h, i: (0,) * len(shape))
    return pl.pallas_call(
        _attn_a_kernel,
        grid=(b, A_HEADS, s // tq),
        in_specs=[
            small((1, A_HEAD_DIM)), small((1, A_HEAD_DIM)), small((1, A_HEAD_DIM)), small((1, A_HEAD_DIM)),
            small((2 * A_HEAD_DIM, 1)),
            pl.BlockSpec((1, 2 * A_HEAD_DIM, tq), lambda bi, h, i: (bi, h, i)),
            pl.BlockSpec((1, s, 2 * A_HEAD_DIM), lambda bi, h, i: (bi, 0, h)),
            pl.BlockSpec((1, s // tk, A_VROWS, tk), lambda bi, h, i: (bi, 0, h, 0)),
            pl.BlockSpec((1, 1, kn.shape[2], LANES), lambda bi, h, i: (bi, h, 0, 0)),
        ],
        out_specs=pl.BlockSpec((1, tq, 2 * A_HEAD_DIM), lambda bi, h, i: (bi, i, h)),
        out_shape=jax.ShapeDtypeStruct((b, s, A_W), BF16),
        scratch_shapes=[pltpu.VMEM((KEY_BLOCK, 2 * tq), F32), pltpu.VMEM((KEY_BLOCK, 2 * tq), F32),
                        pltpu.VMEM((A_VROWS, 2 * tq), F32)],
        compiler_params=pltpu.CompilerParams(
            dimension_semantics=("arbitrary", "arbitrary", "arbitrary"), vmem_limit_bytes=VMEM_LIMIT),
        name="attn_a",
    )(lq1, lk1, lq2, lk2, g_diff, qat, ka, vat, kn)


def _attention_b(qbt, kb, vbt, kn):
    b, _, s = qbt.shape
    tk = vbt.shape[3]
    tq = min(B_Q_TILE, s)
    rows = B_REP * B_HEAD_DIM
    return pl.pallas_call(
        _attn_b_kernel,
        grid=(b, B_KV_HEADS, s // tq),
        in_specs=[
            pl.BlockSpec((1, rows, tq), lambda bi, g, i: (bi, g, i)),
            pl.BlockSpec((1, 1, s, LANES), lambda bi, g, i: (bi, g, 0, 0)),
            pl.BlockSpec((1, s // tk, B_VROWS, tk), lambda bi, g, i: (bi, 0, g, 0)),
            pl.BlockSpec((1, 1, kn.shape[2], LANES), lambda bi, g, i: (bi, g, 0, 0)),
        ],
        out_specs=pl.BlockSpec((1, tq, rows), lambda bi, g, i: (bi, i, g)),
        out_shape=jax.ShapeDtypeStruct((b, s, B_Q), BF16),
        scratch_shapes=[pltpu.VMEM((KEY_BLOCK, B_REP * tq), F32), pltpu.VMEM((KEY_BLOCK, B_REP * tq), F32),
                        pltpu.VMEM((B_VROWS, B_REP * tq), F32)],
        compiler_params=pltpu.CompilerParams(
            dimension_semantics=("arbitrary", "arbitrary", "arbitrary"), vmem_limit_bytes=VMEM_LIMIT),
        name="attn_b",
    )(qbt, kb, vbt, kn)


def _merge_kernel(x_ref, oa_ref, ob_ref, ga_ref, gb_ref, wa_ref, wb_ref, wo_ref, h_ref):
    ya = jnp.dot(oa_ref[0], wa_ref[...], preferred_element_type=F32)
    yb = jnp.dot(ob_ref[0], wb_ref[...], preferred_element_type=F32)
    merged = ga_ref[0] * ya + gb_ref[0] * yb
    h_ref[0] = x_ref[0] + jnp.dot(merged.astype(BF16), wo_ref[...], preferred_element_type=F32)


def _merge(x, oa, ob, ga, gb, wa, wb, wo):
    b, s, d = x.shape
    t = min(MERGE_TILE, s)
    tile = pl.BlockSpec((1, t, d), lambda bi, i: (bi, i, 0))
    return pl.pallas_call(
        _merge_kernel,
        grid=(b, s // t),
        in_specs=[tile, tile, tile, tile, tile,
                  _const_spec(wa.shape), _const_spec(wb.shape), _const_spec(wo.shape)],
        out_specs=tile,
        out_shape=jax.ShapeDtypeStruct((b, s, d), F32),
        compiler_params=pltpu.CompilerParams(
            dimension_semantics=("arbitrary", "arbitrary"), vmem_limit_bytes=VMEM_LIMIT),
        name="merge",
    )(x, oa, ob, ga, gb, wa, wb, wo)


def _ffn_kernel(hp_ref, h_ref, hn_ref, p_ref, gffn_ref, wup_ref, cw_ref, cb_ref, wdn_ref,
                wple_ref, gple_ref, wpg_ref, gfin_ref, y_ref, ug_ref, uv_ref, a_ref):
    h = h_ref[0]
    t = h.shape[0]
    rows = t + 2 * HALO
    first_row = pl.program_id(1) * t
    seq_len = pl.num_programs(1) * t
    hh = jnp.concatenate([hp_ref[0], h, hn_ref[0]], axis=0)
    r = lax.broadcasted_iota(jnp.int32, (rows, 1), 0) + (first_row - HALO)
    inside = jnp.logical_and(r >= 0, r < seq_len)
    n2 = jnp.where(inside, _rms(hh, gffn_ref[...]), 0.0).astype(BF16)

    def up_proj(c):
        c0 = c * FFN_CHUNK
        ug_ref[c % 2] = jnp.dot(n2, wup_ref[:, c0:c0 + FFN_CHUNK], preferred_element_type=F32)
        uv_ref[c % 2] = jnp.dot(n2, wup_ref[:, D_FF + c0:D_FF + c0 + FFN_CHUNK],
                                preferred_element_type=F32)

    def conv(u_ref, slot, c0):
        w = cw_ref[:, c0:c0 + FFN_CHUNK]
        um = u_ref[slot, HALO - 1:HALO - 1 + t, :]
        uc = u_ref[slot, HALO:HALO + t, :]
        up = u_ref[slot, HALO + 1:HALO + 1 + t, :]
        return um * w[0:1] + uc * w[1:2] + up * w[2:3] + cb_ref[:, c0:c0 + FFN_CHUNK]

    n_chunks = D_FF // FFN_CHUNK
    split = (n_chunks + 1) // 2 * FFN_CHUNK
    up_proj(0)
    for c in range(n_chunks):
        if c + 1 < n_chunks:
            up_proj(c + 1)
        c0 = c * FFN_CHUNK
        a = jax.nn.gelu(conv(ug_ref, c % 2, c0)) * conv(uv_ref, c % 2, D_FF + c0)
        a_ref[:, c0:c0 + FFN_CHUNK] = a.astype(BF16)
        if c0 + FFN_CHUNK == split:
            down = jnp.dot(a_ref[:, :split], wdn_ref[:split, :], preferred_element_type=F32)
    down = down + jnp.dot(a_ref[:, split:], wdn_ref[split:, :], preferred_element_type=F32)
    h2 = h + down

    n3 = _rms(h2, gple_ref[...]).astype(BF16)
    gate = jax.nn.sigmoid(jnp.dot(n3, wpg_ref[...], preferred_element_type=F32))
    e = jnp.dot(p_ref[0].astype(BF16), wple_ref[...], preferred_element_type=F32)
    h3 = h2 + e * gate
    y_ref[0] = _rms(h3, gfin_ref[...])


def _ffn(h, p, g_ffn, wup, conv_w, conv_b, wdn, wple, g_ple, wpg, g_final):
    b, s, d = h.shape
    t = min(FFN_TILE, s)
    per = t // HALO
    nhb = s // HALO
    tile = pl.BlockSpec((1, t, d), lambda bi, i: (bi, i, 0))
    return pl.pallas_call(
        _ffn_kernel,
        grid=(b, s // t),
        in_specs=[
            pl.BlockSpec((1, HALO, d), lambda bi, i: (bi, jnp.maximum(i * per - 1, 0), 0)),
            tile,
            pl.BlockSpec((1, HALO, d), lambda bi, i: (bi, jnp.minimum((i + 1) * per, nhb - 1), 0)),
            pl.BlockSpec((1, t, PLE_DIM), lambda bi, i: (bi, i, 0)),
            _const_spec((1, d)),
            _const_spec(wup.shape),
            _const_spec(conv_w.shape),
            _const_spec(conv_b.shape),
            _const_spec(wdn.shape),
            _const_spec(wple.shape),
            _const_spec((1, d)),
            _const_spec(wpg.shape),
            _const_spec((1, d)),
        ],
        out_specs=tile,
        out_shape=jax.ShapeDtypeStruct((b, s, d), F32),
        scratch_shapes=[pltpu.VMEM((2, t + 2 * HALO, FFN_CHUNK), F32),
                        pltpu.VMEM((2, t + 2 * HALO, FFN_CHUNK), F32),
                        pltpu.VMEM((t, D_FF), BF16)],
        compiler_params=pltpu.CompilerParams(
            dimension_semantics=("arbitrary", "arbitrary"), vmem_limit_bytes=VMEM_LIMIT),
        name="ffn",
    )(h, h, h, p, g_ffn, wup, conv_w, conv_b, wdn, wple, g_ple, wpg, g_final)


def _rope_tables_t(s):
    pos = jnp.arange(s, dtype=jnp.int32)

    def tab(ids, dim, theta):
        inv = theta ** (-jnp.arange(0, dim, 2, dtype=F32) / dim)
        ang = ids.astype(F32)[:, None] * inv[None, :]
        return jnp.cos(ang).T, jnp.sin(ang).T

    ca, sa = tab(pos, A_ROT_DIM, A_ROPE_THETA)
    cr, sr = tab(pos // GRID_W, B_HEAD_DIM // 2, B_AXIAL_THETA)
    cc, sc = tab(pos % GRID_W, B_HEAD_DIM // 2, B_AXIAL_THETA)
    return jnp.concatenate([ca, cr, cc], axis=0), jnp.concatenate([sa, sr, sc], axis=0)


def _encode(x, p, cos_t, sin_t, w):
    qat, ka, vat, qbt, kb, vbt, ga, gb, kna, knb = _project(
        x, cos_t, sin_t, w["g_mix"], w["wt"], w["wg"], w["g_qn"], w["g_kn"])
    oa = _attention_a(w["lq1"], w["lk1"], w["lq2"], w["lk2"], w["g_diff"], qat, ka, vat, kna)
    ob = _attention_b(qbt, kb, vbt, knb)
    h1 = _merge(x, oa, ob, ga, gb, w["w_a"], w["w_b"], w["w_out"])
    return _ffn(h1, p, w["g_ffn"], w["w_up"], w["conv_w"], w["conv_b"], w["w_down"],
                w["w_ple"], w["g_ple"], w["w_ple_gate"], w["g_final"])


def kernel(x_prompt, x_sample, p_prompt, p_sample, g_mix, w_in, lambda_q1, lambda_k1, lambda_q2,
           lambda_k2, g_diff, w_a, g_qn, g_kn, w_b, w_out, g_ffn, w_up, conv_w, conv_b, w_down,
           w_ple, g_ple, w_ple_gate, g_final):
    w_in0 = w_in[0]
    row = lambda v: v.reshape(1, -1).astype(F32)
    col = lambda v: v.reshape(-1, 1).astype(F32)
    w = dict(
        g_mix=row(g_mix[0]),
        wt=w_in0[:, :N_TRANSPOSED].T.astype(BF16),
        wg=w_in0[:, N_TRANSPOSED:].astype(BF16),
        g_qn=col(g_qn[0]), g_kn=col(g_kn[0]),
        lq1=row(lambda_q1[0]), lk1=row(lambda_k1[0]), lq2=row(lambda_q2[0]), lk2=row(lambda_k2[0]),
        g_diff=col(g_diff[0]),
        w_a=w_a[0].astype(BF16), w_b=w_b[0].astype(BF16), w_out=w_out[0].astype(BF16),
        g_ffn=row(g_ffn[0]),
        w_up=w_up[0].astype(BF16), conv_w=conv_w[0].astype(F32), conv_b=row(conv_b[0]),
        w_down=w_down[0].astype(BF16),
        w_ple=w_ple[0].astype(BF16), g_ple=row(g_ple[0]), w_ple_gate=w_ple_gate[0].astype(BF16),
        g_final=row(g_final),
    )
    outs = []
    for x, p in ((x_prompt, p_prompt), (x_sample, p_sample)):
        cos_t, sin_t = _rope_tables_t(x.shape[1])
        outs.append(_encode(x, p[0], cos_t, sin_t, w))
    return tuple(outs)
```

```python
import functools
import math

import jax
import jax.numpy as jnp
from jax import lax
from jax.experimental import pallas as pl
from jax.experimental.pallas import tpu as pltpu

F32 = jnp.float32
BF16 = jnp.bfloat16

D_MODEL = 1024
PLE_DIM = 256
GRID_W = 64
EPS = 1e-6
A_HEADS = 8
A_HEAD_DIM = 64
A_ROT_DIM = A_HEAD_DIM // 4
A_ROPE_THETA = 500000.0
B_HEADS = 16
B_KV_HEADS = 4
B_HEAD_DIM = 64
B_REP = B_HEADS // B_KV_HEADS
B_AXIAL_THETA = 10000.0
D_FF = 2816
LAM_INIT = 0.8 - 0.6 * math.exp(-0.3 * 0)

A_W = A_HEADS * 2 * A_HEAD_DIM
B_Q = B_HEADS * B_HEAD_DIM
B_KV = B_KV_HEADS * B_HEAD_DIM
N_TRANSPOSED = 3 * A_W + B_Q + 2 * B_KV

V7X_VMEM_BYTES = 64 * 1024 * 1024
VMEM_LIMIT = 56 * 1024 * 1024
SUBLANES = 8
LANES = 128

TOKEN_TILE = 512
A_Q_TILE = 512
B_Q_TILE = 256
MERGE_TILE = 512
FFN_TILE = 512
FFN_CHUNK = 256
HALO = SUBLANES
SUM_ROWS = 16
A_VROWS = 2 * A_HEAD_DIM + SUM_ROWS
B_VROWS = B_HEAD_DIM + SUM_ROWS
NEG_BIG = -1e30


def _rms(x, g):
    ms = jnp.mean(x * x, axis=-1, keepdims=True)
    return (x * lax.rsqrt(ms + EPS)) * g


def _rms_rows(x, g):
    ms = jnp.mean(x * x, axis=0, keepdims=True)
    return (x * lax.rsqrt(ms + EPS)) * g


def _proj_kernel(x_ref, cos_ref, sin_ref, gmix_ref, wt_ref, wg_ref, gq_ref, gk_ref,
                 qat_ref, ka_ref, vat_ref, qbt_ref, kb_ref, vbt_ref, ga_ref, gb_ref,
                 qna_ref, kna_ref, qnb_ref, knb_ref):
    x = x_ref[0]
    n = _rms(x, gmix_ref[...])
    nb = n.astype(BF16)
    nt = n.T.astype(BF16)

    cos = cos_ref[...]
    sin = sin_ref[...]
    ca, sa = cos[0:8], sin[0:8]
    cr, sr = cos[8:24], sin[8:24]
    cc, sc = cos[24:40], sin[24:40]

    def rope_a(p):
        x1, x2 = p[0:8], p[8:16]
        return jnp.concatenate([x1 * ca - x2 * sa, x2 * ca + x1 * sa, p[16:64]], axis=0)

    def rope_b(p):
        a1, a2, b1, b2 = p[0:16], p[16:32], p[32:48], p[48:64]
        return jnp.concatenate([a1 * cr - a2 * sr, a2 * cr + a1 * sr,
                                b1 * cc - b2 * sc, b2 * cc + b1 * sc], axis=0)

    def proj_t(lo, hi):
        return jnp.dot(wt_ref[lo:hi, :], nt, preferred_element_type=F32)

    def max_sq_norm(*vecs_t):
        sq = [jnp.max(jnp.sum(v * v, axis=0, keepdims=True)) for v in vecs_t]
        return jnp.full((SUBLANES, LANES), functools.reduce(jnp.maximum, sq), F32)

    scale = A_HEAD_DIM ** -0.5 * math.log2(math.e)

    ones_rows = (lax.broadcasted_iota(jnp.int32, (SUM_ROWS, nt.shape[1]), 0) == 0).astype(BF16)

    def put_aq(p):
        for h in range(A_HEADS):
            q1 = rope_a(p[h * 128:h * 128 + 64]) * scale
            q2 = rope_a(p[h * 128 + 64:(h + 1) * 128]) * scale
            qat_ref[0, h * 128:(h + 1) * 128, :] = jnp.concatenate([q1, q2], axis=0).astype(BF16)
            qna_ref[0, h] = max_sq_norm(q1, q2)

    def put_ak(p):
        for s in range(A_W // LANES):
            blk = jnp.concatenate([rope_a(p[s * 128:s * 128 + 64]),
                                   rope_a(p[s * 128 + 64:(s + 1) * 128])], axis=0)
            ka_ref[0, :, s * LANES:(s + 1) * LANES] = blk.T.astype(BF16)
            kna_ref[0, s] = max_sq_norm(blk)

    def put_v(p, v_ref, heads, dv):
        vrows = dv + SUM_ROWS
        for h in range(heads):
            v_ref[0, 0, h * vrows:h * vrows + dv, :] = p[h * dv:(h + 1) * dv].astype(BF16)
            v_ref[0, 0, h * vrows + dv:(h + 1) * vrows, :] = ones_rows

    def put_bq(p):
        gq = gq_ref[...]
        for g in range(B_KV_HEADS):
            ys = [rope_b(_rms_rows(p[h * 64:(h + 1) * 64], gq)) * scale
                  for h in range(g * B_REP, (g + 1) * B_REP)]
            qbt_ref[0, g * B_REP * 64:(g + 1) * B_REP * 64, :] = jnp.concatenate(ys, axis=0).astype(BF16)
            qnb_ref[0, g] = max_sq_norm(*ys)

    def put_bk(p):
        gk = gk_ref[...]
        zeros = jnp.zeros((LANES - B_HEAD_DIM, p.shape[1]), F32)
        for g in range(B_KV_HEADS):
            y = rope_b(_rms_rows(p[g * 64:(g + 1) * 64], gk))
            kb_ref[0, g] = jnp.concatenate([y, zeros], axis=0).T.astype(BF16)
            knb_ref[0, g] = max_sq_norm(y)

    off_bq = 3 * A_W
    off_bk = off_bq + B_Q
    p_aq = proj_t(0, A_W)
    p_ak = proj_t(A_W, 2 * A_W)
    put_aq(p_aq)
    p_av = proj_t(2 * A_W, off_bq)
    put_ak(p_ak)
    p_bq = proj_t(off_bq, off_bk)
    put_v(p_av, vat_ref, A_HEADS, 2 * A_HEAD_DIM)
    p_bkv = proj_t(off_bk, off_bk + 2 * B_KV)
    put_bq(p_bq)
    gates = jnp.dot(nb, wg_ref[...], preferred_element_type=F32)
    put_bk(p_bkv[:B_KV])
    put_v(p_bkv[B_KV:], vbt_ref, B_KV_HEADS, B_HEAD_DIM)
    ga_ref[0] = jax.nn.sigmoid(gates[:, :D_MODEL])
    gb_ref[0] = jax.nn.sigmoid(gates[:, D_MODEL:])


def _const_spec(shape):
    nd = len(shape)
    return pl.BlockSpec(shape, lambda *_: (0,) * nd, pipeline_mode=pl.Buffered(1))


def _project(x, cos_t, sin_t, g_mix, wt, wg, gq, gk):
    b, s, d = x.shape
    t = min(TOKEN_TILE, s)
    nt = s // t
    out_shape = (
        jax.ShapeDtypeStruct((b, A_W, s), BF16),
        jax.ShapeDtypeStruct((b, s, A_W), BF16),
        jax.ShapeDtypeStruct((b, nt, A_HEADS * A_VROWS, t), BF16),
        jax.ShapeDtypeStruct((b, B_Q, s), BF16),
        jax.ShapeDtypeStruct((b, B_KV_HEADS, s, LANES), BF16),
        jax.ShapeDtypeStruct((b, nt, B_KV_HEADS * B_VROWS, t), BF16),
        jax.ShapeDtypeStruct((b, s, D_MODEL), F32),
        jax.ShapeDtypeStruct((b, s, D_MODEL), F32),
        jax.ShapeDtypeStruct((b, A_HEADS, nt * SUBLANES, LANES), F32),
        jax.ShapeDtypeStruct((b, A_HEADS, nt * SUBLANES, LANES), F32),
        jax.ShapeDtypeStruct((b, B_KV_HEADS, nt * SUBLANES, LANES), F32),
        jax.ShapeDtypeStruct((b, B_KV_HEADS, nt * SUBLANES, LANES), F32),
    )
    in_specs = [
        pl.BlockSpec((1, t, d), lambda bi, i: (bi, i, 0)),
        pl.BlockSpec((40, t), lambda bi, i: (0, i)),
        pl.BlockSpec((40, t), lambda bi, i: (0, i)),
        _const_spec((1, d)),
        _const_spec(wt.shape),
        _const_spec(wg.shape),
        _const_spec((B_HEAD_DIM, 1)),
        _const_spec((B_HEAD_DIM, 1)),
    ]
    out_specs = (
        pl.BlockSpec((1, A_W, t), lambda bi, i: (bi, 0, i)),
        pl.BlockSpec((1, t, A_W), lambda bi, i: (bi, i, 0)),
        pl.BlockSpec((1, 1, A_HEADS * A_VROWS, t), lambda bi, i: (bi, i, 0, 0)),
        pl.BlockSpec((1, B_Q, t), lambda bi, i: (bi, 0, i)),
        pl.BlockSpec((1, B_KV_HEADS, t, LANES), lambda bi, i: (bi, 0, i, 0)),
        pl.BlockSpec((1, 1, B_KV_HEADS * B_VROWS, t), lambda bi, i: (bi, i, 0, 0)),
        pl.BlockSpec((1, t, D_MODEL), lambda bi, i: (bi, i, 0)),
        pl.BlockSpec((1, t, D_MODEL), lambda bi, i: (bi, i, 0)),
        pl.BlockSpec((1, A_HEADS, SUBLANES, LANES), lambda bi, i: (bi, 0, i, 0)),
        pl.BlockSpec((1, A_HEADS, SUBLANES, LANES), lambda bi, i: (bi, 0, i, 0)),
        pl.BlockSpec((1, B_KV_HEADS, SUBLANES, LANES), lambda bi, i: (bi, 0, i, 0)),
        pl.BlockSpec((1, B_KV_HEADS, SUBLANES, LANES), lambda bi, i: (bi, 0, i, 0)),
    )
    return pl.pallas_call(
        _proj_kernel,
        grid=(b, nt),
        in_specs=in_specs,
        out_specs=out_specs,
        out_shape=out_shape,
        compiler_params=pltpu.CompilerParams(
            dimension_semantics=("arbitrary", "arbitrary"), vmem_limit_bytes=VMEM_LIMIT),
        name="proj",
    )(x, cos_t, sin_t, g_mix, wt, wg, gq, gk)


KEY_BLOCK = 256


def _flash_t(q_t, k_rows, v_tile_t, s_refs, acc_ref, n_tiles, track_max):
    s_a, s_b = s_refs
    tk = s_a.shape[0]
    acc_ref[...] = jnp.zeros_like(acc_ref)

    def scores(j, s_ref):
        s = jnp.dot(k_rows(pl.multiple_of(j * tk, tk), tk), q_t, preferred_element_type=F32)
        s_ref[...] = s
        return jnp.max(s, axis=0, keepdims=True) if track_max else None

    def consume(v_t, s_ref, m, bmax):
        if not track_max:
            p = jnp.exp2(s_ref[...]).astype(BF16)
            acc_ref[...] += jnp.dot(v_t, p, preferred_element_type=F32)
            return m
        m_new = jnp.maximum(m, bmax)
        alpha = jnp.exp2(m - m_new)
        p = jnp.exp2(s_ref[...] - m_new).astype(BF16)
        acc_ref[...] = alpha * acc_ref[...] + jnp.dot(v_t, p, preferred_element_type=F32)
        return m_new

    def tile(i, carry, last):
        m, bmax_a = carry
        v = v_tile_t(i)
        assert v.shape[1] == 2 * tk
        bmax_b = scores(2 * i + 1, s_b)
        m = consume(v[:, :tk], s_a, m, bmax_a)
        if not last:
            bmax_a = scores(2 * i + 2, s_a)
        m = consume(v[:, tk:], s_b, m, bmax_b)
        return m, bmax_a

    bmax = scores(0, s_a)
    carry = (jnp.full((1, q_t.shape[1]), NEG_BIG, F32), bmax) if track_max else (None, None)
    n_loop = n_tiles - 1
    unroll = next(u for u in (5, 3, 1) if n_loop % u == 0)
    carry = lax.fori_loop(0, n_loop, lambda i, c: tile(i, c, False), carry, unroll=unroll)
    tile(n_tiles - 1, carry, True)


SCORE_BOUND = 64.0
BOUND_SLACK = 1.05


def _attend(q_t, q_tile, k_rows, v_tile_t, qn_ref, kn_ref, s_refs, acc_ref, n_tiles):
    b, h = pl.program_id(0), pl.program_id(1)
    q_sq = qn_ref[b, h, q_tile]
    k_sq = functools.reduce(jnp.maximum, [kn_ref[b, h, t] for t in range(n_tiles)])
    bounded = q_sq * k_sq * (BOUND_SLACK * BOUND_SLACK) <= SCORE_BOUND * SCORE_BOUND

    @pl.when(bounded)
    def _():
        _flash_t(q_t, k_rows, v_tile_t, s_refs, acc_ref, n_tiles, track_max=False)

    @pl.when(jnp.logical_not(bounded))
    def _():
        _flash_t(q_t, k_rows, v_tile_t, s_refs, acc_ref, n_tiles, track_max=True)


Q_SUBTILES = 2


def _sub_tile(q_block, sub, tq):
    assert Q_SUBTILES == 2
    return jnp.where(sub == 0, q_block[:, :tq], q_block[:, tq:])


def _attn_a_kernel(qn_ref, kn_ref, lq1_ref, lk1_ref, lq2_ref, lk2_ref, gd_ref, qt_ref, k_ref, vt_ref,
                   o_ref, s_a, s_b, acc_ref):
    tq = qt_ref.shape[2] // Q_SUBTILES

    def sub_tile(sub, carry):
        q = _sub_tile(qt_ref[0], sub, tq)
        row = lax.broadcasted_iota(jnp.int32, q.shape, 0)
        zero = jnp.zeros_like(q)
        q_t = jnp.concatenate([jnp.where(row < A_HEAD_DIM, q, zero),
                               jnp.where(row >= A_HEAD_DIM, q, zero)], axis=1)
        _attend(q_t, (pl.program_id(2) * Q_SUBTILES + sub) * tq // vt_ref.shape[3],
                lambda r0, n: k_ref[0, pl.ds(r0, n), :], lambda i: vt_ref[0, i],
                qn_ref, kn_ref, (s_a, s_b), acc_ref, vt_ref.shape[1])

        lam = (jnp.exp(jnp.sum(lq1_ref[...] * lk1_ref[...], axis=-1, keepdims=True))
               - jnp.exp(jnp.sum(lq2_ref[...] * lk2_ref[...], axis=-1, keepdims=True))
               + LAM_INIT)
        dv = 2 * A_HEAD_DIM
        acc = acc_ref[0:dv, :]
        l = acc_ref[dv:dv + 1, :]
        o = acc[:, :tq] / l[:, :tq] - lam * (acc[:, tq:] / l[:, tq:])
        y = _rms_rows(o, gd_ref[...]) * (1.0 - LAM_INIT)
        o_ref[0, pl.ds(pl.multiple_of(sub * tq, tq), tq), :] = y.T.astype(BF16)
        return carry

    lax.fori_loop(0, Q_SUBTILES, sub_tile, 0)


def _attn_b_kernel(qn_ref, kn_ref, qt_ref, k_ref, vt_ref, o_ref, s_a, s_b, acc_ref):
    tq = qt_ref.shape[2] // Q_SUBTILES

    def sub_tile(sub, carry):
        q = _sub_tile(qt_ref[0], sub, tq)
        q_cat = jnp.concatenate([q[r * 64:(r + 1) * 64] for r in range(B_REP)], axis=1)
        q_t = jnp.concatenate([q_cat, jnp.zeros_like(q_cat)], axis=0)
        _attend(q_t, (pl.program_id(2) * Q_SUBTILES + sub) * tq // vt_ref.shape[3],
                lambda r0, n: k_ref[0, 0, pl.ds(r0, n), :], lambda i: vt_ref[0, i],
                qn_ref, kn_ref, (s_a, s_b), acc_ref, vt_ref.shape[1])

        o = acc_ref[0:B_HEAD_DIM, :] / acc_ref[B_HEAD_DIM:B_HEAD_DIM + 1, :]
        o_rows = jnp.concatenate([o[:, r * tq:(r + 1) * tq] for r in range(B_REP)], axis=0)
        o_ref[0, pl.ds(pl.multiple_of(sub * tq, tq), tq), :] = o_rows.T.astype(BF16)
        return carry

    lax.fori_loop(0, Q_SUBTILES, sub_tile, 0)


def _tile_scalars(stat):
    return stat[:, :, ::SUBLANES, 0]


def _attention_a(lq1, lk1, lq2, lk2, g_diff, qat, ka, vat, qn, kn):
    b, _, s = qat.shape
    tk = vat.shape[3]
    tq = A_Q_TILE
    tstep = Q_SUBTILES * tq
    small = lambda shape: pl.BlockSpec(shape, lambda bi, h, i, *_: (0,) * len(shape))
    return pl.pallas_call(
        _attn_a_kernel,
        grid_spec=pltpu.PrefetchScalarGridSpec(
            num_scalar_prefetch=2,
            grid=(b, A_HEADS, s // tstep),
            in_specs=[
                small((1, A_HEAD_DIM)), small((1, A_HEAD_DIM)), small((1, A_HEAD_DIM)), small((1, A_HEAD_DIM)),
                small((2 * A_HEAD_DIM, 1)),
                pl.BlockSpec((1, 2 * A_HEAD_DIM, tstep), lambda bi, h, i, *_: (bi, h, i)),
                pl.BlockSpec((1, s, 2 * A_HEAD_DIM), lambda bi, h, i, *_: (bi, 0, h)),
                pl.BlockSpec((1, s // tk, A_VROWS, tk), lambda bi, h, i, *_: (bi, 0, h, 0)),
            ],
            out_specs=pl.BlockSpec((1, tstep, 2 * A_HEAD_DIM), lambda bi, h, i, *_: (bi, i, h)),
            scratch_shapes=[pltpu.VMEM((KEY_BLOCK, 2 * tq), F32), pltpu.VMEM((KEY_BLOCK, 2 * tq), F32),
                            pltpu.VMEM((A_VROWS, 2 * tq), F32)]),
        out_shape=jax.ShapeDtypeStruct((b, s, A_W), BF16),
        compiler_params=pltpu.CompilerParams(
            dimension_semantics=("arbitrary", "arbitrary", "arbitrary"), vmem_limit_bytes=VMEM_LIMIT),
        name="attn_a",
    )(_tile_scalars(qn), _tile_scalars(kn), lq1, lk1, lq2, lk2, g_diff, qat, ka, vat)


def _attention_b(qbt, kb, vbt, qn, kn):
    b, _, s = qbt.shape
    tk = vbt.shape[3]
    tq = B_Q_TILE
    tstep = Q_SUBTILES * tq
    rows = B_REP * B_HEAD_DIM
    return pl.pallas_call(
        _attn_b_kernel,
        grid_spec=pltpu.PrefetchScalarGridSpec(
            num_scalar_prefetch=2,
            grid=(b, B_KV_HEADS, s // tstep),
            in_specs=[
                pl.BlockSpec((1, rows, tstep), lambda bi, g, i, *_: (bi, g, i)),
                pl.BlockSpec((1, 1, s, LANES), lambda bi, g, i, *_: (bi, g, 0, 0)),
                pl.BlockSpec((1, s // tk, B_VROWS, tk), lambda bi, g, i, *_: (bi, 0, g, 0)),
            ],
            out_specs=pl.BlockSpec((1, tstep, rows), lambda bi, g, i, *_: (bi, i, g)),
            scratch_shapes=[pltpu.VMEM((KEY_BLOCK, B_REP * tq), F32), pltpu.VMEM((KEY_BLOCK, B_REP * tq), F32),
                            pltpu.VMEM((B_VROWS, B_REP * tq), F32)]),
        out_shape=jax.ShapeDtypeStruct((b, s, B_Q), BF16),
        compiler_params=pltpu.CompilerParams(
            dimension_semantics=("arbitrary", "arbitrary", "arbitrary"), vmem_limit_bytes=VMEM_LIMIT),
        name="attn_b",
    )(_tile_scalars(qn), _tile_scalars(kn), qbt, kb, vbt)


def _merge_kernel(x_ref, oa_ref, ob_ref, ga_ref, gb_ref, wa_ref, wb_ref, wo_ref, h_ref):
    ya = jnp.dot(oa_ref[0], wa_ref[...], preferred_element_type=F32)
    yb = jnp.dot(ob_ref[0], wb_ref[...], preferred_element_type=F32)
    merged = ga_ref[0] * ya + gb_ref[0] * yb
    h_ref[0] = x_ref[0] + jnp.dot(merged.astype(BF16), wo_ref[...], preferred_element_type=F32)


def _merge(x, oa, ob, ga, gb, wa, wb, wo):
    b, s, d = x.shape
    t = min(MERGE_TILE, s)
    tile = pl.BlockSpec((1, t, d), lambda bi, i: (bi, i, 0))
    return pl.pallas_call(
        _merge_kernel,
        grid=(b, s // t),
        in_specs=[tile, tile, tile, tile, tile,
                  _const_spec(wa.shape), _const_spec(wb.shape), _const_spec(wo.shape)],
        out_specs=tile,
        out_shape=jax.ShapeDtypeStruct((b, s, d), F32),
        compiler_params=pltpu.CompilerParams(
            dimension_semantics=("arbitrary", "arbitrary"), vmem_limit_bytes=VMEM_LIMIT),
        name="merge",
    )(x, oa, ob, ga, gb, wa, wb, wo)


def _ffn_kernel(hp_ref, h_ref, hn_ref, p_ref, gffn_ref, wup_ref, cw_ref, cb_ref, wdn_ref,
                wple_ref, gple_ref, wpg_ref, gfin_ref, y_ref, ug_ref, uv_ref, a_ref):
    h = h_ref[0]
    t = h.shape[0]
    rows = t + 2 * HALO
    first_row = pl.program_id(1) * t
    seq_len = pl.num_programs(1) * t
    hh = jnp.concatenate([hp_ref[0], h, hn_ref[0]], axis=0)
    r = lax.broadcasted_iota(jnp.int32, (rows, 1), 0) + (first_row - HALO)
    inside = jnp.logical_and(r >= 0, r < seq_len)
    n2 = jnp.where(inside, _rms(hh, gffn_ref[...]), 0.0).astype(BF16)

    def up_proj(c):
        c0 = c * FFN_CHUNK
        ug_ref[c % 2] = jnp.dot(n2, wup_ref[:, c0:c0 + FFN_CHUNK], preferred_element_type=F32)
        uv_ref[c % 2] = jnp.dot(n2, wup_ref[:, D_FF + c0:D_FF + c0 + FFN_CHUNK],
                                preferred_element_type=F32)

    def conv(u_ref, slot, c0):
        w = cw_ref[:, c0:c0 + FFN_CHUNK]
        um = u_ref[slot, HALO - 1:HALO - 1 + t, :]
        uc = u_ref[slot, HALO:HALO + t, :]
        up = u_ref[slot, HALO + 1:HALO + 1 + t, :]
        return um * w[0:1] + uc * w[1:2] + up * w[2:3] + cb_ref[:, c0:c0 + FFN_CHUNK]

    n_chunks = D_FF // FFN_CHUNK
    split = (n_chunks + 1) // 2 * FFN_CHUNK
    up_proj(0)
    for c in range(n_chunks):
        if c + 1 < n_chunks:
            up_proj(c + 1)
        c0 = c * FFN_CHUNK
        a = jax.nn.gelu(conv(ug_ref, c % 2, c0)) * conv(uv_ref, c % 2, D_FF + c0)
        a_ref[:, c0:c0 + FFN_CHUNK] = a.astype(BF16)
        if c0 + FFN_CHUNK == split:
            down = jnp.dot(a_ref[:, :split], wdn_ref[:split, :], preferred_element_type=F32)
    down = down + jnp.dot(a_ref[:, split:], wdn_ref[split:, :], preferred_element_type=F32)
    h2 = h + down

    n3 = _rms(h2, gple_ref[...]).astype(BF16)
    gate = jax.nn.sigmoid(jnp.dot(n3, wpg_ref[...], preferred_element_type=F32))
    e = jnp.dot(p_ref[0].astype(BF16), wple_ref[...], preferred_element_type=F32)
    h3 = h2 + e * gate
    y_ref[0] = _rms(h3, gfin_ref[...])


def _ffn(h, p, g_ffn, wup, conv_w, conv_b, wdn, wple, g_ple, wpg, g_final):
    b, s, d = h.shape
    t = min(FFN_TILE, s)
    per = t // HALO
    nhb = s // HALO
    tile = pl.BlockSpec((1, t, d), lambda bi, i: (bi, i, 0))
    return pl.pallas_call(
        _ffn_kernel,
        grid=(b, s // t),
        in_specs=[
            pl.BlockSpec((1, HALO, d), lambda bi, i: (bi, jnp.maximum(i * per - 1, 0), 0)),
            tile,
            pl.BlockSpec((1, HALO, d), lambda bi, i: (bi, jnp.minimum((i + 1) * per, nhb - 1), 0)),
            pl.BlockSpec((1, t, PLE_DIM), lambda bi, i: (bi, i, 0)),
            _const_spec((1, d)),
            _const_spec(wup.shape),
            _const_spec(conv_w.shape),
            _const_spec(conv_b.shape),
            _const_spec(wdn.shape),
            _const_spec(wple.shape),
            _const_spec((1, d)),
            _const_spec(wpg.shape),
            _const_spec((1, d)),
        ],
        out_specs=tile,
        out_shape=jax.ShapeDtypeStruct((b, s, d), F32),
        scratch_shapes=[pltpu.VMEM((2, t + 2 * HALO, FFN_CHUNK), F32),
                        pltpu.VMEM((2, t + 2 * HALO, FFN_CHUNK), F32),
                        pltpu.VMEM((t, D_FF), BF16)],
        compiler_params=pltpu.CompilerParams(
            dimension_semantics=("arbitrary", "arbitrary"), vmem_limit_bytes=VMEM_LIMIT),
        name="ffn",
    )(h, h, h, p, g_ffn, wup, conv_w, conv_b, wdn, wple, g_ple, wpg, g_final)


def _rope_tables_t(s):
    pos = jnp.arange(s, dtype=jnp.int32)

    def tab(ids, dim, theta):
        inv = theta ** (-jnp.arange(0, dim, 2, dtype=F32) / dim)
        ang = ids.astype(F32)[:, None] * inv[None, :]
        return jnp.cos(ang).T, jnp.sin(ang).T

    ca, sa = tab(pos, A_ROT_DIM, A_ROPE_THETA)
    cr, sr = tab(pos // GRID_W, B_HEAD_DIM // 2, B_AXIAL_THETA)
    cc, sc = tab(pos % GRID_W, B_HEAD_DIM // 2, B_AXIAL_THETA)
    return jnp.concatenate([ca, cr, cc], axis=0), jnp.concatenate([sa, sr, sc], axis=0)


def _encode(x, p, cos_t, sin_t, w):
    qat, ka, vat, qbt, kb, vbt, ga, gb, qna, kna, qnb, knb = _project(
        x, cos_t, sin_t, w["g_mix"], w["wt"], w["wg"], w["g_qn"], w["g_kn"])
    oa = _attention_a(w["lq1"], w["lk1"], w["lq2"], w["lk2"], w["g_diff"], qat, ka, vat, qna, kna)
    ob = _attention_b(qbt, kb, vbt, qnb, knb)
    h1 = _merge(x, oa, ob, ga, gb, w["w_a"], w["w_b"], w["w_out"])
    return _ffn(h1, p, w["g_ffn"], w["w_up"], w["conv_w"], w["conv_b"], w["w_down"],
                w["w_ple"], w["g_ple"], w["w_ple_gate"], w["g_final"])


def kernel(x_prompt, x_sample, p_prompt, p_sample, g_mix, w_in, lambda_q1, lambda_k1, lambda_q2,
           lambda_k2, g_diff, w_a, g_qn, g_kn, w_b, w_out, g_ffn, w_up, conv_w, conv_b, w_down,
           w_ple, g_ple, w_ple_gate, g_final):
    w_in0 = w_in[0]
    row = lambda v: v.reshape(1, -1).astype(F32)
    col = lambda v: v.reshape(-1, 1).astype(F32)
    w = dict(
        g_mix=row(g_mix[0]),
        wt=w_in0[:, :N_TRANSPOSED].T.astype(BF16),
        wg=w_in0[:, N_TRANSPOSED:].astype(BF16),
        g_qn=col(g_qn[0]), g_kn=col(g_kn[0]),
        lq1=row(lambda_q1[0]), lk1=row(lambda_k1[0]), lq2=row(lambda_q2[0]), lk2=row(lambda_k2[0]),
        g_diff=col(g_diff[0]),
        w_a=w_a[0].astype(BF16), w_b=w_b[0].astype(BF16), w_out=w_out[0].astype(BF16),
        g_ffn=row(g_ffn[0]),
        w_up=w_up[0].astype(BF16), conv_w=conv_w[0].astype(F32), conv_b=row(conv_b[0]),
        w_down=w_down[0].astype(BF16),
        w_ple=w_ple[0].astype(BF16), g_ple=row(g_ple[0]), w_ple_gate=w_ple_gate[0].astype(BF16),
        g_final=row(g_final),
    )
    outs = []
    for x, p in ((x_prompt, p_prompt), (x_sample, p_sample)):
        cos_t, sin_t = _rope_tables_t(x.shape[1])
        outs.append(_encode(x, p[0], cos_t, sin_t, w))
    return tuple(outs)
```

```python
import functools
import math

import jax
import jax.numpy as jnp
from jax import lax
from jax.experimental import pallas as pl
from jax.experimental.pallas import tpu as pltpu

F32 = jnp.float32
BF16 = jnp.bfloat16

D_MODEL = 1024
PLE_DIM = 256
GRID_W = 64
EPS = 1e-6
A_HEADS = 8
A_HEAD_DIM = 64
A_ROT_DIM = A_HEAD_DIM // 4
A_ROPE_THETA = 500000.0
B_HEADS = 16
B_KV_HEADS = 4
B_HEAD_DIM = 64
B_REP = B_HEADS // B_KV_HEADS
B_AXIAL_THETA = 10000.0
D_FF = 2816
LAM_INIT = 0.8 - 0.6 * math.exp(-0.3 * 0)

A_W = A_HEADS * 2 * A_HEAD_DIM
B_Q = B_HEADS * B_HEAD_DIM
B_KV = B_KV_HEADS * B_HEAD_DIM
N_TRANSPOSED = 3 * A_W + B_Q + 2 * B_KV

V7X_VMEM_BYTES = 64 * 1024 * 1024
VMEM_LIMIT = 56 * 1024 * 1024
SUBLANES = 8
LANES = 128

TOKEN_TILE = 512
A_Q_TILE = 512
B_Q_TILE = 256
MERGE_TILE = 512
FFN_TILE = 512
FFN_CHUNK = 256
HALO = SUBLANES
SUM_ROWS = 16
A_VROWS = 2 * A_HEAD_DIM + SUM_ROWS
B_VROWS = B_HEAD_DIM + SUM_ROWS
NEG_BIG = -1e30


def _rms(x, g):
    ms = jnp.mean(x * x, axis=-1, keepdims=True)
    return (x * lax.rsqrt(ms + EPS)) * g


def _rms_rows(x, g):
    ms = jnp.mean(x * x, axis=0, keepdims=True)
    return (x * lax.rsqrt(ms + EPS)) * g


def _proj_kernel(x_ref, cos_ref, sin_ref, gmix_ref, wt_ref, wg_ref, gq_ref, gk_ref,
                 qat_ref, ka_ref, vat_ref, qbt_ref, kb_ref, vbt_ref, ga_ref, gb_ref,
                 qna_ref, kna_ref, qnb_ref, knb_ref):
    x = x_ref[0]
    n = _rms(x, gmix_ref[...])
    nb = n.astype(BF16)
    nt = n.T.astype(BF16)

    cos = cos_ref[...]
    sin = sin_ref[...]
    ca, sa = cos[0:8], sin[0:8]
    cr, sr = cos[8:24], sin[8:24]
    cc, sc = cos[24:40], sin[24:40]

    def rope_a(p):
        x1, x2 = p[0:8], p[8:16]
        return jnp.concatenate([x1 * ca - x2 * sa, x2 * ca + x1 * sa, p[16:64]], axis=0)

    def rope_b(p):
        a1, a2, b1, b2 = p[0:16], p[16:32], p[32:48], p[48:64]
        return jnp.concatenate([a1 * cr - a2 * sr, a2 * cr + a1 * sr,
                                b1 * cc - b2 * sc, b2 * cc + b1 * sc], axis=0)

    def proj_t(lo, hi):
        return jnp.dot(wt_ref[lo:hi, :], nt, preferred_element_type=F32)

    def max_sq_norm(*vecs_t):
        sq = [jnp.max(jnp.sum(v * v, axis=0, keepdims=True)) for v in vecs_t]
        return jnp.full((SUBLANES, LANES), functools.reduce(jnp.maximum, sq), F32)

    scale = A_HEAD_DIM ** -0.5 * math.log2(math.e)

    ones_rows = (lax.broadcasted_iota(jnp.int32, (SUM_ROWS, nt.shape[1]), 0) == 0).astype(BF16)

    def put_aq(p):
        for h in range(A_HEADS):
            q1 = rope_a(p[h * 128:h * 128 + 64]) * scale
            q2 = rope_a(p[h * 128 + 64:(h + 1) * 128]) * scale
            qat_ref[0, h * 128:(h + 1) * 128, :] = jnp.concatenate([q1, q2], axis=0).astype(BF16)
            qna_ref[0, h] = max_sq_norm(q1, q2)

    def put_ak(p):
        for s in range(A_W // LANES):
            blk = jnp.concatenate([rope_a(p[s * 128:s * 128 + 64]),
                                   rope_a(p[s * 128 + 64:(s + 1) * 128])], axis=0)
            ka_ref[0, :, s * LANES:(s + 1) * LANES] = blk.T.astype(BF16)
            kna_ref[0, s] = max_sq_norm(blk)

    def put_v(p, v_ref, heads, dv):
        vrows = dv + SUM_ROWS
        for h in range(heads):
            v_ref[0, 0, h * vrows:h * vrows + dv, :] = p[h * dv:(h + 1) * dv].astype(BF16)
            v_ref[0, 0, h * vrows + dv:(h + 1) * vrows, :] = ones_rows

    def put_bq(p):
        gq = gq_ref[...]
        for g in range(B_KV_HEADS):
            ys = [rope_b(_rms_rows(p[h * 64:(h + 1) * 64], gq)) * scale
                  for h in range(g * B_REP, (g + 1) * B_REP)]
            qbt_ref[0, g * B_REP * 64:(g + 1) * B_REP * 64, :] = jnp.concatenate(ys, axis=0).astype(BF16)
            qnb_ref[0, g] = max_sq_norm(*ys)

    def put_bk(p):
        gk = gk_ref[...]
        zeros = jnp.zeros((LANES - B_HEAD_DIM, p.shape[1]), F32)
        for g in range(B_KV_HEADS):
            y = rope_b(_rms_rows(p[g * 64:(g + 1) * 64], gk))
            kb_ref[0, g] = jnp.concatenate([y, zeros], axis=0).T.astype(BF16)
            knb_ref[0, g] = max_sq_norm(y)

    off_bq = 3 * A_W
    off_bk = off_bq + B_Q
    p_aq = proj_t(0, A_W)
    p_ak = proj_t(A_W, 2 * A_W)
    put_aq(p_aq)
    p_av = proj_t(2 * A_W, off_bq)
    put_ak(p_ak)
    p_bq = proj_t(off_bq, off_bk)
    put_v(p_av, vat_ref, A_HEADS, 2 * A_HEAD_DIM)
    p_bkv = proj_t(off_bk, off_bk + 2 * B_KV)
    put_bq(p_bq)
    gates = jnp.dot(nb, wg_ref[...], preferred_element_type=F32)
    put_bk(p_bkv[:B_KV])
    put_v(p_bkv[B_KV:], vbt_ref, B_KV_HEADS, B_HEAD_DIM)
    ga_ref[0] = jax.nn.sigmoid(gates[:, :D_MODEL])
    gb_ref[0] = jax.nn.sigmoid(gates[:, D_MODEL:])


def _const_spec(shape):
    nd = len(shape)
    return pl.BlockSpec(shape, lambda *_: (0,) * nd, pipeline_mode=pl.Buffered(1))


def _project(x, cos_t, sin_t, g_mix, wt, wg, gq, gk):
    b, s, d = x.shape
    t = min(TOKEN_TILE, s)
    nt = s // t
    out_shape = (
        jax.ShapeDtypeStruct((b, A_W, s), BF16),
        jax.ShapeDtypeStruct((b, s, A_W), BF16),
        jax.ShapeDtypeStruct((b, nt, A_HEADS * A_VROWS, t), BF16),
        jax.ShapeDtypeStruct((b, B_Q, s), BF16),
        jax.ShapeDtypeStruct((b, B_KV_HEADS, s, LANES), BF16),
        jax.ShapeDtypeStruct((b, nt, B_KV_HEADS * B_VROWS, t), BF16),
        jax.ShapeDtypeStruct((b, s, D_MODEL), F32),
        jax.ShapeDtypeStruct((b, s, D_MODEL), F32),
        jax.ShapeDtypeStruct((b, A_HEADS, nt * SUBLANES, LANES), F32),
        jax.ShapeDtypeStruct((b, A_HEADS, nt * SUBLANES, LANES), F32),
        jax.ShapeDtypeStruct((b, B_KV_HEADS, nt * SUBLANES, LANES), F32),
        jax.ShapeDtypeStruct((b, B_KV_HEADS, nt * SUBLANES, LANES), F32),
    )
    in_specs = [
        pl.BlockSpec((1, t, d), lambda bi, i: (bi, i, 0)),
        pl.BlockSpec((40, t), lambda bi, i: (0, i)),
        pl.BlockSpec((40, t), lambda bi, i: (0, i)),
        _const_spec((1, d)),
        _const_spec(wt.shape),
        _const_spec(wg.shape),
        _const_spec((B_HEAD_DIM, 1)),
        _const_spec((B_HEAD_DIM, 1)),
    ]
    out_specs = (
        pl.BlockSpec((1, A_W, t), lambda bi, i: (bi, 0, i)),
        pl.BlockSpec((1, t, A_W), lambda bi, i: (bi, i, 0)),
        pl.BlockSpec((1, 1, A_HEADS * A_VROWS, t), lambda bi, i: (bi, i, 0, 0)),
        pl.BlockSpec((1, B_Q, t), lambda bi, i: (bi, 0, i)),
        pl.BlockSpec((1, B_KV_HEADS, t, LANES), lambda bi, i: (bi, 0, i, 0)),
        pl.BlockSpec((1, 1, B_KV_HEADS * B_VROWS, t), lambda bi, i: (bi, i, 0, 0)),
        pl.BlockSpec((1, t, D_MODEL), lambda bi, i: (bi, i, 0)),
        pl.BlockSpec((1, t, D_MODEL), lambda bi, i: (bi, i, 0)),
        pl.BlockSpec((1, A_HEADS, SUBLANES, LANES), lambda bi, i: (bi, 0, i, 0)),
        pl.BlockSpec((1, A_HEADS, SUBLANES, LANES), lambda bi, i: (bi, 0, i, 0)),
        pl.BlockSpec((1, B_KV_HEADS, SUBLANES, LANES), lambda bi, i: (bi, 0, i, 0)),
        pl.BlockSpec((1, B_KV_HEADS, SUBLANES, LANES), lambda bi, i: (bi, 0, i, 0)),
    )
    return pl.pallas_call(
        _proj_kernel,
        grid=(b, nt),
        in_specs=in_specs,
        out_specs=out_specs,
        out_shape=out_shape,
        compiler_params=pltpu.CompilerParams(
            dimension_semantics=("arbitrary", "arbitrary"), vmem_limit_bytes=VMEM_LIMIT),
        name="proj",
    )(x, cos_t, sin_t, g_mix, wt, wg, gq, gk)


KEY_BLOCK = 256


def _flash_t(q_t, k_rows, v_tile_t, s_refs, acc_ref, n_tiles, track_max):
    s_a, s_b = s_refs
    tk = s_a.shape[0]
    acc_ref[...] = jnp.zeros_like(acc_ref)

    def scores(j, s_ref):
        s = jnp.dot(k_rows(pl.multiple_of(j * tk, tk), tk), q_t, preferred_element_type=F32)
        s_ref[...] = s
        return jnp.max(s, axis=0, keepdims=True) if track_max else None

    def consume(v_t, s_ref, m, bmax):
        if not track_max:
            p = jnp.exp2(s_ref[...]).astype(BF16)
            acc_ref[...] += jnp.dot(v_t, p, preferred_element_type=F32)
            return m
        m_new = jnp.maximum(m, bmax)
        alpha = jnp.exp2(m - m_new)
        p = jnp.exp2(s_ref[...] - m_new).astype(BF16)
        acc_ref[...] = alpha * acc_ref[...] + jnp.dot(v_t, p, preferred_element_type=F32)
        return m_new

    def tile(i, carry, last):
        m, bmax_a = carry
        v = v_tile_t(i)
        assert v.shape[1] == 2 * tk
        bmax_b = scores(2 * i + 1, s_b)
        m = consume(v[:, :tk], s_a, m, bmax_a)
        if not last:
            bmax_a = scores(2 * i + 2, s_a)
        m = consume(v[:, tk:], s_b, m, bmax_b)
        return m, bmax_a

    bmax = scores(0, s_a)
    carry = (jnp.full((1, q_t.shape[1]), NEG_BIG, F32), bmax) if track_max else (None, None)
    n_loop = n_tiles - 1
    unroll = next(u for u in ((5, 3, 1) if track_max else (15, 5, 3, 1)) if n_loop % u == 0)
    carry = lax.fori_loop(0, n_loop, lambda i, c: tile(i, c, False), carry, unroll=unroll)
    tile(n_tiles - 1, carry, True)


SCORE_BOUND = 64.0
BOUND_SLACK = 1.05


def _attend(q_t, q_tile, k_rows, v_tile_t, qn_ref, kn_ref, s_refs, acc_ref, n_tiles):
    b, h = pl.program_id(0), pl.program_id(1)
    q_sq = qn_ref[b, h, q_tile]
    k_sq = functools.reduce(jnp.maximum, [kn_ref[b, h, t] for t in range(n_tiles)])
    bounded = q_sq * k_sq * (BOUND_SLACK * BOUND_SLACK) <= SCORE_BOUND * SCORE_BOUND

    @pl.when(bounded)
    def _():
        _flash_t(q_t, k_rows, v_tile_t, s_refs, acc_ref, n_tiles, track_max=False)

    @pl.when(jnp.logical_not(bounded))
    def _():
        _flash_t(q_t, k_rows, v_tile_t, s_refs, acc_ref, n_tiles, track_max=True)


Q_SUBTILES = 2


def _sub_tile(q_block, sub, tq):
    assert Q_SUBTILES == 2
    return jnp.where(sub == 0, q_block[:, :tq], q_block[:, tq:])


def _attn_a_kernel(qn_ref, kn_ref, lq1_ref, lk1_ref, lq2_ref, lk2_ref, gd_ref, qt_ref, k_ref, vt_ref,
                   o_ref, s_a, s_b, acc_ref):
    tq = qt_ref.shape[2] // Q_SUBTILES

    def sub_tile(sub, carry):
        q = _sub_tile(qt_ref[0], sub, tq)
        row = lax.broadcasted_iota(jnp.int32, q.shape, 0)
        zero = jnp.zeros_like(q)
        q_t = jnp.concatenate([jnp.where(row < A_HEAD_DIM, q, zero),
                               jnp.where(row >= A_HEAD_DIM, q, zero)], axis=1)
        _attend(q_t, (pl.program_id(2) * Q_SUBTILES + sub) * tq // vt_ref.shape[3],
                lambda r0, n: k_ref[0, pl.ds(r0, n), :], lambda i: vt_ref[0, i],
                qn_ref, kn_ref, (s_a, s_b), acc_ref, vt_ref.shape[1])

        lam = (jnp.exp(jnp.sum(lq1_ref[...] * lk1_ref[...], axis=-1, keepdims=True))
               - jnp.exp(jnp.sum(lq2_ref[...] * lk2_ref[...], axis=-1, keepdims=True))
               + LAM_INIT)
        dv = 2 * A_HEAD_DIM
        acc = acc_ref[0:dv, :]
        l = acc_ref[dv:dv + 1, :]
        o = acc[:, :tq] / l[:, :tq] - lam * (acc[:, tq:] / l[:, tq:])
        y = _rms_rows(o, gd_ref[...]) * (1.0 - LAM_INIT)
        o_ref[0, pl.ds(pl.multiple_of(sub * tq, tq), tq), :] = y.T.astype(BF16)
        return carry

    lax.fori_loop(0, Q_SUBTILES, sub_tile, 0)


def _attn_b_kernel(qn_ref, kn_ref, qt_ref, k_ref, vt_ref, o_ref, s_a, s_b, acc_ref):
    tq = qt_ref.shape[2] // Q_SUBTILES

    def sub_tile(sub, carry):
        q = _sub_tile(qt_ref[0], sub, tq)
        q_cat = jnp.concatenate([q[r * 64:(r + 1) * 64] for r in range(B_REP)], axis=1)
        q_t = jnp.concatenate([q_cat, jnp.zeros_like(q_cat)], axis=0)
        _attend(q_t, (pl.program_id(2) * Q_SUBTILES + sub) * tq // vt_ref.shape[3],
                lambda r0, n: k_ref[0, 0, pl.ds(r0, n), :], lambda i: vt_ref[0, i],
                qn_ref, kn_ref, (s_a, s_b), acc_ref, vt_ref.shape[1])

        o = acc_ref[0:B_HEAD_DIM, :] / acc_ref[B_HEAD_DIM:B_HEAD_DIM + 1, :]
        o_rows = jnp.concatenate([o[:, r * tq:(r + 1) * tq] for r in range(B_REP)], axis=0)
        o_ref[0, pl.ds(pl.multiple_of(sub * tq, tq), tq), :] = o_rows.T.astype(BF16)
        return carry

    lax.fori_loop(0, Q_SUBTILES, sub_tile, 0)


def _tile_scalars(stat):
    return stat[:, :, ::SUBLANES, 0]


def _attention_a(lq1, lk1, lq2, lk2, g_diff, qat, ka, vat, qn, kn):
    b, _, s = qat.shape
    tk = vat.shape[3]
    tq = A_Q_TILE
    tstep = Q_SUBTILES * tq
    small = lambda shape: pl.BlockSpec(shape, lambda bi, h, i, *_: (0,) * len(shape))
    return pl.pallas_call(
        _attn_a_kernel,
        grid_spec=pltpu.PrefetchScalarGridSpec(
            num_scalar_prefetch=2,
            grid=(b, A_HEADS, s // tstep),
            in_specs=[
                small((1, A_HEAD_DIM)), small((1, A_HEAD_DIM)), small((1, A_HEAD_DIM)), small((1, A_HEAD_DIM)),
                small((2 * A_HEAD_DIM, 1)),
                pl.BlockSpec((1, 2 * A_HEAD_DIM, tstep), lambda bi, h, i, *_: (bi, h, i)),
                pl.BlockSpec((1, s, 2 * A_HEAD_DIM), lambda bi, h, i, *_: (bi, 0, h)),
                pl.BlockSpec((1, s // tk, A_VROWS, tk), lambda bi, h, i, *_: (bi, 0, h, 0)),
            ],
            out_specs=pl.BlockSpec((1, tstep, 2 * A_HEAD_DIM), lambda bi, h, i, *_: (bi, i, h)),
            scratch_shapes=[pltpu.VMEM((KEY_BLOCK, 2 * tq), F32), pltpu.VMEM((KEY_BLOCK, 2 * tq), F32),
                            pltpu.VMEM((A_VROWS, 2 * tq), F32)]),
        out_shape=jax.ShapeDtypeStruct((b, s, A_W), BF16),
        compiler_params=pltpu.CompilerParams(
            dimension_semantics=("arbitrary", "arbitrary", "arbitrary"), vmem_limit_bytes=VMEM_LIMIT),
        name="attn_a",
    )(_tile_scalars(qn), _tile_scalars(kn), lq1, lk1, lq2, lk2, g_diff, qat, ka, vat)


def _attention_b(qbt, kb, vbt, qn, kn):
    b, _, s = qbt.shape
    tk = vbt.shape[3]
    tq = B_Q_TILE
    tstep = Q_SUBTILES * tq
    rows = B_REP * B_HEAD_DIM
    return pl.pallas_call(
        _attn_b_kernel,
        grid_spec=pltpu.PrefetchScalarGridSpec(
            num_scalar_prefetch=2,
            grid=(b, B_KV_HEADS, s // tstep),
            in_specs=[
                pl.BlockSpec((1, rows, tstep), lambda bi, g, i, *_: (bi, g, i)),
                pl.BlockSpec((1, 1, s, LANES), lambda bi, g, i, *_: (bi, g, 0, 0)),
                pl.BlockSpec((1, s // tk, B_VROWS, tk), lambda bi, g, i, *_: (bi, 0, g, 0)),
            ],
            out_specs=pl.BlockSpec((1, tstep, rows), lambda bi, g, i, *_: (bi, i, g)),
            scratch_shapes=[pltpu.VMEM((KEY_BLOCK, B_REP * tq), F32), pltpu.VMEM((KEY_BLOCK, B_REP * tq), F32),
                            pltpu.VMEM((B_VROWS, B_REP * tq), F32)]),
        out_shape=jax.ShapeDtypeStruct((b, s, B_Q), BF16),
        compiler_params=pltpu.CompilerParams(
            dimension_semantics=("arbitrary", "arbitrary", "arbitrary"), vmem_limit_bytes=VMEM_LIMIT),
        name="attn_b",
    )(_tile_scalars(qn), _tile_scalars(kn), qbt, kb, vbt)


def _merge_kernel(x_ref, oa_ref, ob_ref, ga_ref, gb_ref, wa_ref, wb_ref, wo_ref, h_ref):
    ya = jnp.dot(oa_ref[0], wa_ref[...], preferred_element_type=F32)
    yb = jnp.dot(ob_ref[0], wb_ref[...], preferred_element_type=F32)
    merged = ga_ref[0] * ya + gb_ref[0] * yb
    h_ref[0] = x_ref[0] + jnp.dot(merged.astype(BF16), wo_ref[...], preferred_element_type=F32)


def _merge(x, oa, ob, ga, gb, wa, wb, wo):
    b, s, d = x.shape
    t = min(MERGE_TILE, s)
    tile = pl.BlockSpec((1, t, d), lambda bi, i: (bi, i, 0))
    return pl.pallas_call(
        _merge_kernel,
        grid=(b, s // t),
        in_specs=[tile, tile, tile, tile, tile,
                  _const_spec(wa.shape), _const_spec(wb.shape), _const_spec(wo.shape)],
        out_specs=tile,
        out_shape=jax.ShapeDtypeStruct((b, s, d), F32),
        compiler_params=pltpu.CompilerParams(
            dimension_semantics=("arbitrary", "arbitrary"), vmem_limit_bytes=VMEM_LIMIT),
        name="merge",
    )(x, oa, ob, ga, gb, wa, wb, wo)


def _ffn_kernel(hp_ref, h_ref, hn_ref, p_ref, gffn_ref, wup_ref, cw_ref, cb_ref, wdn_ref,
                wple_ref, gple_ref, wpg_ref, gfin_ref, y_ref, ug_ref, uv_ref, a_ref):
    h = h_ref[0]
    t = h.shape[0]
    rows = t + 2 * HALO
    first_row = pl.program_id(1) * t
    seq_len = pl.num_programs(1) * t
    hh = jnp.concatenate([hp_ref[0], h, hn_ref[0]], axis=0)
    r = lax.broadcasted_iota(jnp.int32, (rows, 1), 0) + (first_row - HALO)
    inside = jnp.logical_and(r >= 0, r < seq_len)
    n2 = jnp.where(inside, _rms(hh, gffn_ref[...]), 0.0).astype(BF16)

    def up_proj(c):
        c0 = c * FFN_CHUNK
        ug_ref[c % 2] = jnp.dot(n2, wup_ref[:, c0:c0 + FFN_CHUNK], preferred_element_type=F32)
        uv_ref[c % 2] = jnp.dot(n2, wup_ref[:, D_FF + c0:D_FF + c0 + FFN_CHUNK],
                                preferred_element_type=F32)

    def conv(u_ref, slot, c0):
        w = cw_ref[:, c0:c0 + FFN_CHUNK]
        um = u_ref[slot, HALO - 1:HALO - 1 + t, :]
        uc = u_ref[slot, HALO:HALO + t, :]
        up = u_ref[slot, HALO + 1:HALO + 1 + t, :]
        return um * w[0:1] + uc * w[1:2] + up * w[2:3] + cb_ref[:, c0:c0 + FFN_CHUNK]

    n_chunks = D_FF // FFN_CHUNK
    split = (n_chunks + 1) // 2 * FFN_CHUNK
    up_proj(0)
    for c in range(n_chunks):
        if c + 1 < n_chunks:
            up_proj(c + 1)
        c0 = c * FFN_CHUNK
        a = jax.nn.gelu(conv(ug_ref, c % 2, c0)) * conv(uv_ref, c % 2, D_FF + c0)
        a_ref[:, c0:c0 + FFN_CHUNK] = a.astype(BF16)
        if c0 + FFN_CHUNK == split:
            down = jnp.dot(a_ref[:, :split], wdn_ref[:split, :], preferred_element_type=F32)
    down = down + jnp.dot(a_ref[:, split:], wdn_ref[split:, :], preferred_element_type=F32)
    h2 = h + down

    n3 = _rms(h2, gple_ref[...]).astype(BF16)
    gate = jax.nn.sigmoid(jnp.dot(n3, wpg_ref[...], preferred_element_type=F32))
    e = jnp.dot(p_ref[0].astype(BF16), wple_ref[...], preferred_element_type=F32)
    h3 = h2 + e * gate
    y_ref[0] = _rms(h3, gfin_ref[...])


def _ffn(h, p, g_ffn, wup, conv_w, conv_b, wdn, wple, g_ple, wpg, g_final):
    b, s, d = h.shape
    t = min(FFN_TILE, s)
    per = t // HALO
    nhb = s // HALO
    tile = pl.BlockSpec((1, t, d), lambda bi, i: (bi, i, 0))
    return pl.pallas_call(
        _ffn_kernel,
        grid=(b, s // t),
        in_specs=[
            pl.BlockSpec((1, HALO, d), lambda bi, i: (bi, jnp.maximum(i * per - 1, 0), 0)),
            tile,
            pl.BlockSpec((1, HALO, d), lambda bi, i: (bi, jnp.minimum((i + 1) * per, nhb - 1), 0)),
            pl.BlockSpec((1, t, PLE_DIM), lambda bi, i: (bi, i, 0)),
            _const_spec((1, d)),
            _const_spec(wup.shape),
            _const_spec(conv_w.shape),
            _const_spec(conv_b.shape),
            _const_spec(wdn.shape),
            _const_spec(wple.shape),
            _const_spec((1, d)),
            _const_spec(wpg.shape),
            _const_spec((1, d)),
        ],
        out_specs=tile,
        out_shape=jax.ShapeDtypeStruct((b, s, d), F32),
        scratch_shapes=[pltpu.VMEM((2, t + 2 * HALO, FFN_CHUNK), F32),
                        pltpu.VMEM((2, t + 2 * HALO, FFN_CHUNK), F32),
                        pltpu.VMEM((t, D_FF), BF16)],
        compiler_params=pltpu.CompilerParams(
            dimension_semantics=("arbitrary", "arbitrary"), vmem_limit_bytes=VMEM_LIMIT),
        name="ffn",
    )(h, h, h, p, g_ffn, wup, conv_w, conv_b, wdn, wple, g_ple, wpg, g_final)


def _rope_tables_t(s):
    pos = jnp.arange(s, dtype=jnp.int32)

    def tab(ids, dim, theta):
        inv = theta ** (-jnp.arange(0, dim, 2, dtype=F32) / dim)
        ang = ids.astype(F32)[:, None] * inv[None, :]
        return jnp.cos(ang).T, jnp.sin(ang).T

    ca, sa = tab(pos, A_ROT_DIM, A_ROPE_THETA)
    cr, sr = tab(pos // GRID_W, B_HEAD_DIM // 2, B_AXIAL_THETA)
    cc, sc = tab(pos % GRID_W, B_HEAD_DIM // 2, B_AXIAL_THETA)
    return jnp.concatenate([ca, cr, cc], axis=0), jnp.concatenate([sa, sr, sc], axis=0)


def _encode(x, p, cos_t, sin_t, w):
    qat, ka, vat, qbt, kb, vbt, ga, gb, qna, kna, qnb, knb = _project(
        x, cos_t, sin_t, w["g_mix"], w["wt"], w["wg"], w["g_qn"], w["g_kn"])
    oa = _attention_a(w["lq1"], w["lk1"], w["lq2"], w["lk2"], w["g_diff"], qat, ka, vat, qna, kna)
    ob = _attention_b(qbt, kb, vbt, qnb, knb)
    h1 = _merge(x, oa, ob, ga, gb, w["w_a"], w["w_b"], w["w_out"])
    return _ffn(h1, p, w["g_ffn"], w["w_up"], w["conv_w"], w["conv_b"], w["w_down"],
                w["w_ple"], w["g_ple"], w["w_ple_gate"], w["g_final"])


def kernel(x_prompt, x_sample, p_prompt, p_sample, g_mix, w_in, lambda_q1, lambda_k1, lambda_q2,
           lambda_k2, g_diff, w_a, g_qn, g_kn, w_b, w_out, g_ffn, w_up, conv_w, conv_b, w_down,
           w_ple, g_ple, w_ple_gate, g_final):
    w_in0 = w_in[0]
    row = lambda v: v.reshape(1, -1).astype(F32)
    col = lambda v: v.reshape(-1, 1).astype(F32)
    w = dict(
        g_mix=row(g_mix[0]),
        wt=w_in0[:, :N_TRANSPOSED].T.astype(BF16),
        wg=w_in0[:, N_TRANSPOSED:].astype(BF16),
        g_qn=col(g_qn[0]), g_kn=col(g_kn[0]),
        lq1=row(lambda_q1[0]), lk1=row(lambda_k1[0]), lq2=row(lambda_q2[0]), lk2=row(lambda_k2[0]),
        g_diff=col(g_diff[0]),
        w_a=w_a[0].astype(BF16), w_b=w_b[0].astype(BF16), w_out=w_out[0].astype(BF16),
        g_ffn=row(g_ffn[0]),
        w_up=w_up[0].astype(BF16), conv_w=conv_w[0].astype(F32), conv_b=row(conv_b[0]),
        w_down=w_down[0].astype(BF16),
        w_ple=w_ple[0].astype(BF16), g_ple=row(g_ple[0]), w_ple_gate=w_ple_gate[0].astype(BF16),
        g_final=row(g_final),
    )
    outs = []
    for x, p in ((x_prompt, p_prompt), (x_sample, p_sample)):
        cos_t, sin_t = _rope_tables_t(x.shape[1])
        outs.append(_encode(x, p[0], cos_t, sin_t, w))
    return tuple(outs)
```

```python
import functools
import math

import jax
import jax.numpy as jnp
from jax import lax
from jax.experimental import pallas as pl
from jax.experimental.pallas import tpu as pltpu

F32 = jnp.float32
BF16 = jnp.bfloat16

D_MODEL = 1024
PLE_DIM = 256
GRID_W = 64
EPS = 1e-6
A_HEADS = 8
A_HEAD_DIM = 64
A_ROT_DIM = A_HEAD_DIM // 4
A_ROPE_THETA = 500000.0
B_HEADS = 16
B_KV_HEADS = 4
B_HEAD_DIM = 64
B_REP = B_HEADS // B_KV_HEADS
B_AXIAL_THETA = 10000.0
D_FF = 2816
LAM_INIT = 0.8 - 0.6 * math.exp(-0.3 * 0)

A_W = A_HEADS * 2 * A_HEAD_DIM
B_Q = B_HEADS * B_HEAD_DIM
B_KV = B_KV_HEADS * B_HEAD_DIM
N_TRANSPOSED = 3 * A_W + B_Q + 2 * B_KV

V7X_VMEM_BYTES = 64 * 1024 * 1024
VMEM_LIMIT = 56 * 1024 * 1024
SUBLANES = 8
LANES = 128

TOKEN_TILE = 512
A_Q_TILE = 512
B_Q_TILE = 256
MERGE_TILE = 512
FFN_TILE = 512
FFN_CHUNK = 256
HALO = SUBLANES
SUM_ROWS = 16
A_VROWS = 2 * A_HEAD_DIM + SUM_ROWS
B_VROWS = B_HEAD_DIM + SUM_ROWS
NEG_BIG = -1e30


def _rms(x, g):
    ms = jnp.mean(x * x, axis=-1, keepdims=True)
    return (x * lax.rsqrt(ms + EPS)) * g


def _rms_rows(x, g):
    ms = jnp.mean(x * x, axis=0, keepdims=True)
    return (x * lax.rsqrt(ms + EPS)) * g


def _proj_kernel(x_ref, cos_ref, sin_ref, gmix_ref, wt_ref, wg_ref, gq_ref, gk_ref,
                 qat_ref, ka_ref, vat_ref, qbt_ref, kb_ref, vbt_ref, ga_ref, gb_ref,
                 qna_ref, kna_ref, qnb_ref, knb_ref):
    x = x_ref[0]
    n = _rms(x, gmix_ref[...])
    nb = n.astype(BF16)
    nt = n.T.astype(BF16)

    cos = cos_ref[...]
    sin = sin_ref[...]
    ca, sa = cos[0:8], sin[0:8]
    cr, sr = cos[8:24], sin[8:24]
    cc, sc = cos[24:40], sin[24:40]

    def rope_a(p):
        x1, x2 = p[0:8], p[8:16]
        return jnp.concatenate([x1 * ca - x2 * sa, x2 * ca + x1 * sa, p[16:64]], axis=0)

    def rope_b(p):
        a1, a2, b1, b2 = p[0:16], p[16:32], p[32:48], p[48:64]
        return jnp.concatenate([a1 * cr - a2 * sr, a2 * cr + a1 * sr,
                                b1 * cc - b2 * sc, b2 * cc + b1 * sc], axis=0)

    def proj_t(lo, hi):
        return jnp.dot(wt_ref[lo:hi, :], nt, preferred_element_type=F32)

    def max_sq_norm(*vecs_t):
        sq = [jnp.max(jnp.sum(v * v, axis=0, keepdims=True)) for v in vecs_t]
        return jnp.full((SUBLANES, LANES), functools.reduce(jnp.maximum, sq), F32)

    scale = A_HEAD_DIM ** -0.5 * math.log2(math.e)

    ones_rows = (lax.broadcasted_iota(jnp.int32, (SUM_ROWS, nt.shape[1]), 0) == 0).astype(BF16)

    def put_aq(p):
        for h in range(A_HEADS):
            q1 = rope_a(p[h * 128:h * 128 + 64]) * scale
            q2 = rope_a(p[h * 128 + 64:(h + 1) * 128]) * scale
            qat_ref[0, h * 128:(h + 1) * 128, :] = jnp.concatenate([q1, q2], axis=0).astype(BF16)
            qna_ref[0, h] = max_sq_norm(q1, q2)

    def put_ak(p):
        for s in range(A_W // LANES):
            blk = jnp.concatenate([rope_a(p[s * 128:s * 128 + 64]),
                                   rope_a(p[s * 128 + 64:(s + 1) * 128])], axis=0)
            ka_ref[0, :, s * LANES:(s + 1) * LANES] = blk.T.astype(BF16)
            kna_ref[0, s] = max_sq_norm(blk)

    def put_v(p, v_ref, heads, dv):
        vrows = dv + SUM_ROWS
        for h in range(heads):
            v_ref[0, 0, h * vrows:h * vrows + dv, :] = p[h * dv:(h + 1) * dv].astype(BF16)
            v_ref[0, 0, h * vrows + dv:(h + 1) * vrows, :] = ones_rows

    def put_bq(p):
        gq = gq_ref[...]
        for g in range(B_KV_HEADS):
            ys = [rope_b(_rms_rows(p[h * 64:(h + 1) * 64], gq)) * scale
                  for h in range(g * B_REP, (g + 1) * B_REP)]
            qbt_ref[0, g * B_REP * 64:(g + 1) * B_REP * 64, :] = jnp.concatenate(ys, axis=0).astype(BF16)
            qnb_ref[0, g] = max_sq_norm(*ys)

    def put_bk(p):
        gk = gk_ref[...]
        zeros = jnp.zeros((LANES - B_HEAD_DIM, p.shape[1]), F32)
        for g in range(B_KV_HEADS):
            y = rope_b(_rms_rows(p[g * 64:(g + 1) * 64], gk))
            kb_ref[0, g] = jnp.concatenate([y, zeros], axis=0).T.astype(BF16)
            knb_ref[0, g] = max_sq_norm(y)

    off_bq = 3 * A_W
    off_bk = off_bq + B_Q
    p_aq = proj_t(0, A_W)
    p_ak = proj_t(A_W, 2 * A_W)
    put_aq(p_aq)
    p_av = proj_t(2 * A_W, off_bq)
    put_ak(p_ak)
    p_bq = proj_t(off_bq, off_bk)
    put_v(p_av, vat_ref, A_HEADS, 2 * A_HEAD_DIM)
    p_bkv = proj_t(off_bk, off_bk + 2 * B_KV)
    put_bq(p_bq)
    gates = jnp.dot(nb, wg_ref[...], preferred_element_type=F32)
    put_bk(p_bkv[:B_KV])
    put_v(p_bkv[B_KV:], vbt_ref, B_KV_HEADS, B_HEAD_DIM)
    ga_ref[0] = jax.nn.sigmoid(gates[:, :D_MODEL])
    gb_ref[0] = jax.nn.sigmoid(gates[:, D_MODEL:])


def _const_spec(shape):
    nd = len(shape)
    return pl.BlockSpec(shape, lambda *_: (0,) * nd, pipeline_mode=pl.Buffered(1))


def _project(x, cos_t, sin_t, g_mix, wt, wg, gq, gk):
    b, s, d = x.shape
    t = min(TOKEN_TILE, s)
    nt = s // t
    out_shape = (
        jax.ShapeDtypeStruct((b, A_W, s), BF16),
        jax.ShapeDtypeStruct((b, s, A_W), BF16),
        jax.ShapeDtypeStruct((b, nt, A_HEADS * A_VROWS, t), BF16),
        jax.ShapeDtypeStruct((b, B_Q, s), BF16),
        jax.ShapeDtypeStruct((b, B_KV_HEADS, s, LANES), BF16),
        jax.ShapeDtypeStruct((b, nt, B_KV_HEADS * B_VROWS, t), BF16),
        jax.ShapeDtypeStruct((b, s, D_MODEL), F32),
        jax.ShapeDtypeStruct((b, s, D_MODEL), F32),
        jax.ShapeDtypeStruct((b, A_HEADS, nt * SUBLANES, LANES), F32),
        jax.ShapeDtypeStruct((b, A_HEADS, nt * SUBLANES, LANES), F32),
        jax.ShapeDtypeStruct((b, B_KV_HEADS, nt * SUBLANES, LANES), F32),
        jax.ShapeDtypeStruct((b, B_KV_HEADS, nt * SUBLANES, LANES), F32),
    )
    in_specs = [
        pl.BlockSpec((1, t, d), lambda bi, i: (bi, i, 0)),
        pl.BlockSpec((40, t), lambda bi, i: (0, i)),
        pl.BlockSpec((40, t), lambda bi, i: (0, i)),
        _const_spec((1, d)),
        _const_spec(wt.shape),
        _const_spec(wg.shape),
        _const_spec((B_HEAD_DIM, 1)),
        _const_spec((B_HEAD_DIM, 1)),
    ]
    out_specs = (
        pl.BlockSpec((1, A_W, t), lambda bi, i: (bi, 0, i)),
        pl.BlockSpec((1, t, A_W), lambda bi, i: (bi, i, 0)),
        pl.BlockSpec((1, 1, A_HEADS * A_VROWS, t), lambda bi, i: (bi, i, 0, 0)),
        pl.BlockSpec((1, B_Q, t), lambda bi, i: (bi, 0, i)),
        pl.BlockSpec((1, B_KV_HEADS, t, LANES), lambda bi, i: (bi, 0, i, 0)),
        pl.BlockSpec((1, 1, B_KV_HEADS * B_VROWS, t), lambda bi, i: (bi, i, 0, 0)),
        pl.BlockSpec((1, t, D_MODEL), lambda bi, i: (bi, i, 0)),
        pl.BlockSpec((1, t, D_MODEL), lambda bi, i: (bi, i, 0)),
        pl.BlockSpec((1, A_HEADS, SUBLANES, LANES), lambda bi, i: (bi, 0, i, 0)),
        pl.BlockSpec((1, A_HEADS, SUBLANES, LANES), lambda bi, i: (bi, 0, i, 0)),
        pl.BlockSpec((1, B_KV_HEADS, SUBLANES, LANES), lambda bi, i: (bi, 0, i, 0)),
        pl.BlockSpec((1, B_KV_HEADS, SUBLANES, LANES), lambda bi, i: (bi, 0, i, 0)),
    )
    return pl.pallas_call(
        _proj_kernel,
        grid=(b, nt),
        in_specs=in_specs,
        out_specs=out_specs,
        out_shape=out_shape,
        compiler_params=pltpu.CompilerParams(
            dimension_semantics=("arbitrary", "arbitrary"), vmem_limit_bytes=VMEM_LIMIT),
        name="proj",
    )(x, cos_t, sin_t, g_mix, wt, wg, gq, gk)


KEY_BLOCK = 256


def _flash_t(q_t, k_rows, v_tile_t, s_refs, acc_ref, n_tiles, track_max):
    s_a, s_b = s_refs
    tk = s_a.shape[0]
    acc_ref[...] = jnp.zeros_like(acc_ref)

    def scores(j, s_ref):
        s = jnp.dot(k_rows(pl.multiple_of(j * tk, tk), tk), q_t, preferred_element_type=F32)
        s_ref[...] = s
        return jnp.max(s, axis=0, keepdims=True) if track_max else None

    def consume(v_t, s_ref, m, bmax):
        if not track_max:
            p = jnp.exp2(s_ref[...]).astype(BF16)
            acc_ref[...] += jnp.dot(v_t, p, preferred_element_type=F32)
            return m
        m_new = jnp.maximum(m, bmax)
        alpha = jnp.exp2(m - m_new)
        p = jnp.exp2(s_ref[...] - m_new).astype(BF16)
        acc_ref[...] = alpha * acc_ref[...] + jnp.dot(v_t, p, preferred_element_type=F32)
        return m_new

    def tile(i, carry, last):
        m, bmax_a = carry
        v = v_tile_t(i)
        assert v.shape[1] == 2 * tk
        bmax_b = scores(2 * i + 1, s_b)
        m = consume(v[:, :tk], s_a, m, bmax_a)
        if not last:
            bmax_a = scores(2 * i + 2, s_a)
        m = consume(v[:, tk:], s_b, m, bmax_b)
        return m, bmax_a

    bmax = scores(0, s_a)
    carry = (jnp.full((1, q_t.shape[1]), NEG_BIG, F32), bmax) if track_max else (None, None)
    n_loop = n_tiles - 1
    unroll = next(u for u in ((5, 3, 1) if track_max else (15, 5, 3, 1)) if n_loop % u == 0)
    carry = lax.fori_loop(0, n_loop, lambda i, c: tile(i, c, False), carry, unroll=unroll)
    tile(n_tiles - 1, carry, True)


SCORE_BOUND = 64.0
BOUND_SLACK = 1.05


def _both_paths(qn_ref, kn_ref, q_tiles, n_tiles, body):
    b, h = pl.program_id(0), pl.program_id(1)
    q_sq = functools.reduce(jnp.maximum, [qn_ref[b, h, t] for t in q_tiles])
    k_sq = functools.reduce(jnp.maximum, [kn_ref[b, h, t] for t in range(n_tiles)])
    bounded = q_sq * k_sq * (BOUND_SLACK * BOUND_SLACK) <= SCORE_BOUND * SCORE_BOUND
    pl.when(bounded)(functools.partial(body, False))
    pl.when(jnp.logical_not(bounded))(functools.partial(body, True))


Q_SUBTILES = 2


def _attn_a_kernel(qn_ref, kn_ref, lq1_ref, lk1_ref, lq2_ref, lk2_ref, gd_ref, qt_ref, k_ref, vt_ref,
                   o_ref, s_a, s_b, *acc_refs):
    tq = qt_ref.shape[2] // Q_SUBTILES
    n_tiles, tile = vt_ref.shape[1], vt_ref.shape[3]
    first = pl.program_id(2) * Q_SUBTILES

    def body(track_max):
        for sub, acc_ref in enumerate(acc_refs):
            q = qt_ref[0, :, sub * tq:(sub + 1) * tq]
            row = lax.broadcasted_iota(jnp.int32, q.shape, 0)
            zero = jnp.zeros_like(q)
            q_t = jnp.concatenate([jnp.where(row < A_HEAD_DIM, q, zero),
                                   jnp.where(row >= A_HEAD_DIM, q, zero)], axis=1)
            _flash_t(q_t, lambda r0, n: k_ref[0, pl.ds(r0, n), :], lambda i: vt_ref[0, i],
                     (s_a, s_b), acc_ref, n_tiles, track_max)

            lam = (jnp.exp(jnp.sum(lq1_ref[...] * lk1_ref[...], axis=-1, keepdims=True))
                   - jnp.exp(jnp.sum(lq2_ref[...] * lk2_ref[...], axis=-1, keepdims=True))
                   + LAM_INIT)
            dv = 2 * A_HEAD_DIM
            acc = acc_ref[0:dv, :]
            l = acc_ref[dv:dv + 1, :]
            o = acc[:, :tq] / l[:, :tq] - lam * (acc[:, tq:] / l[:, tq:])
            y = _rms_rows(o, gd_ref[...]) * (1.0 - LAM_INIT)
            o_ref[0, sub * tq:(sub + 1) * tq, :] = y.T.astype(BF16)

    _both_paths(qn_ref, kn_ref, [(first + sub) * tq // tile for sub in range(Q_SUBTILES)], n_tiles, body)


def _attn_b_kernel(qn_ref, kn_ref, qt_ref, k_ref, vt_ref, o_ref, s_a, s_b, *acc_refs):
    tq = qt_ref.shape[2] // Q_SUBTILES
    n_tiles, tile = vt_ref.shape[1], vt_ref.shape[3]
    first = pl.program_id(2) * Q_SUBTILES

    def body(track_max):
        for sub, acc_ref in enumerate(acc_refs):
            q = qt_ref[0, :, sub * tq:(sub + 1) * tq]
            q_cat = jnp.concatenate([q[r * 64:(r + 1) * 64] for r in range(B_REP)], axis=1)
            q_t = jnp.concatenate([q_cat, jnp.zeros_like(q_cat)], axis=0)
            _flash_t(q_t, lambda r0, n: k_ref[0, 0, pl.ds(r0, n), :], lambda i: vt_ref[0, i],
                     (s_a, s_b), acc_ref, n_tiles, track_max)

            o = acc_ref[0:B_HEAD_DIM, :] / acc_ref[B_HEAD_DIM:B_HEAD_DIM + 1, :]
            o_rows = jnp.concatenate([o[:, r * tq:(r + 1) * tq] for r in range(B_REP)], axis=0)
            o_ref[0, sub * tq:(sub + 1) * tq, :] = o_rows.T.astype(BF16)

    _both_paths(qn_ref, kn_ref, [(first + sub) * tq // tile for sub in range(Q_SUBTILES)], n_tiles, body)


def _tile_scalars(stat):
    return stat[:, :, ::SUBLANES, 0]


def _attention_a(lq1, lk1, lq2, lk2, g_diff, qat, ka, vat, qn, kn):
    b, _, s = qat.shape
    tk = vat.shape[3]
    tq = A_Q_TILE
    tstep = Q_SUBTILES * tq
    small = lambda shape: pl.BlockSpec(shape, lambda bi, h, i, *_: (0,) * len(shape))
    return pl.pallas_call(
        _attn_a_kernel,
        grid_spec=pltpu.PrefetchScalarGridSpec(
            num_scalar_prefetch=2,
            grid=(b, A_HEADS, s // tstep),
            in_specs=[
                small((1, A_HEAD_DIM)), small((1, A_HEAD_DIM)), small((1, A_HEAD_DIM)), small((1, A_HEAD_DIM)),
                small((2 * A_HEAD_DIM, 1)),
                pl.BlockSpec((1, 2 * A_HEAD_DIM, tstep), lambda bi, h, i, *_: (bi, h, i)),
                pl.BlockSpec((1, s, 2 * A_HEAD_DIM), lambda bi, h, i, *_: (bi, 0, h)),
                pl.BlockSpec((1, s // tk, A_VROWS, tk), lambda bi, h, i, *_: (bi, 0, h, 0)),
            ],
            out_specs=pl.BlockSpec((1, tstep, 2 * A_HEAD_DIM), lambda bi, h, i, *_: (bi, i, h)),
            scratch_shapes=([pltpu.VMEM((KEY_BLOCK, 2 * tq), F32)] * 2
                            + [pltpu.VMEM((A_VROWS, 2 * tq), F32)] * Q_SUBTILES)),
        out_shape=jax.ShapeDtypeStruct((b, s, A_W), BF16),
        compiler_params=pltpu.CompilerParams(
            dimension_semantics=("arbitrary", "arbitrary", "arbitrary"), vmem_limit_bytes=VMEM_LIMIT),
        name="attn_a",
    )(_tile_scalars(qn), _tile_scalars(kn), lq1, lk1, lq2, lk2, g_diff, qat, ka, vat)


def _attention_b(qbt, kb, vbt, qn, kn):
    b, _, s = qbt.shape
    tk = vbt.shape[3]
    tq = B_Q_TILE
    tstep = Q_SUBTILES * tq
    rows = B_REP * B_HEAD_DIM
    return pl.pallas_call(
        _attn_b_kernel,
        grid_spec=pltpu.PrefetchScalarGridSpec(
            num_scalar_prefetch=2,
            grid=(b, B_KV_HEADS, s // tstep),
            in_specs=[
                pl.BlockSpec((1, rows, tstep), lambda bi, g, i, *_: (bi, g, i)),
                pl.BlockSpec((1, 1, s, LANES), lambda bi, g, i, *_: (bi, g, 0, 0)),
                pl.BlockSpec((1, s // tk, B_VROWS, tk), lambda bi, g, i, *_: (bi, 0, g, 0)),
            ],
            out_specs=pl.BlockSpec((1, tstep, rows), lambda bi, g, i, *_: (bi, i, g)),
            scratch_shapes=([pltpu.VMEM((KEY_BLOCK, B_REP * tq), F32)] * 2
                            + [pltpu.VMEM((B_VROWS, B_REP * tq), F32)] * Q_SUBTILES)),
        out_shape=jax.ShapeDtypeStruct((b, s, B_Q), BF16),
        compiler_params=pltpu.CompilerParams(
            dimension_semantics=("arbitrary", "arbitrary", "arbitrary"), vmem_limit_bytes=VMEM_LIMIT),
        name="attn_b",
    )(_tile_scalars(qn), _tile_scalars(kn), qbt, kb, vbt)


def _merge_kernel(x_ref, oa_ref, ob_ref, ga_ref, gb_ref, wa_ref, wb_ref, wo_ref, h_ref):
    ya = jnp.dot(oa_ref[0], wa_ref[...], preferred_element_type=F32)
    yb = jnp.dot(ob_ref[0], wb_ref[...], preferred_element_type=F32)
    merged = ga_ref[0] * ya + gb_ref[0] * yb
    h_ref[0] = x_ref[0] + jnp.dot(merged.astype(BF16), wo_ref[...], preferred_element_type=F32)


def _merge(x, oa, ob, ga, gb, wa, wb, wo):
    b, s, d = x.shape
    t = min(MERGE_TILE, s)
    tile = pl.BlockSpec((1, t, d), lambda bi, i: (bi, i, 0))
    return pl.pallas_call(
        _merge_kernel,
        grid=(b, s // t),
        in_specs=[tile, tile, tile, tile, tile,
                  _const_spec(wa.shape), _const_spec(wb.shape), _const_spec(wo.shape)],
        out_specs=tile,
        out_shape=jax.ShapeDtypeStruct((b, s, d), F32),
        compiler_params=pltpu.CompilerParams(
            dimension_semantics=("arbitrary", "arbitrary"), vmem_limit_bytes=VMEM_LIMIT),
        name="merge",
    )(x, oa, ob, ga, gb, wa, wb, wo)


def _ffn_kernel(hp_ref, h_ref, hn_ref, p_ref, gffn_ref, wup_ref, cw_ref, cb_ref, wdn_ref,
                wple_ref, gple_ref, wpg_ref, gfin_ref, y_ref, ug_ref, uv_ref, a_ref):
    h = h_ref[0]
    t = h.shape[0]
    rows = t + 2 * HALO
    first_row = pl.program_id(1) * t
    seq_len = pl.num_programs(1) * t
    hh = jnp.concatenate([hp_ref[0], h, hn_ref[0]], axis=0)
    r = lax.broadcasted_iota(jnp.int32, (rows, 1), 0) + (first_row - HALO)
    inside = jnp.logical_and(r >= 0, r < seq_len)
    n2 = jnp.where(inside, _rms(hh, gffn_ref[...]), 0.0).astype(BF16)

    def up_proj(c):
        c0 = c * FFN_CHUNK
        ug_ref[c % 2] = jnp.dot(n2, wup_ref[:, c0:c0 + FFN_CHUNK], preferred_element_type=F32)
        uv_ref[c % 2] = jnp.dot(n2, wup_ref[:, D_FF + c0:D_FF + c0 + FFN_CHUNK],
                                preferred_element_type=F32)

    def conv(u_ref, slot, c0):
        w = cw_ref[:, c0:c0 + FFN_CHUNK]
        um = u_ref[slot, HALO - 1:HALO - 1 + t, :]
        uc = u_ref[slot, HALO:HALO + t, :]
        up = u_ref[slot, HALO + 1:HALO + 1 + t, :]
        return um * w[0:1] + uc * w[1:2] + up * w[2:3] + cb_ref[:, c0:c0 + FFN_CHUNK]

    n_chunks = D_FF // FFN_CHUNK
    split = (n_chunks + 1) // 2 * FFN_CHUNK
    up_proj(0)
    for c in range(n_chunks):
        if c + 1 < n_chunks:
            up_proj(c + 1)
        c0 = c * FFN_CHUNK
        a = jax.nn.gelu(conv(ug_ref, c % 2, c0)) * conv(uv_ref, c % 2, D_FF + c0)
        a_ref[:, c0:c0 + FFN_CHUNK] = a.astype(BF16)
        if c0 + FFN_CHUNK == split:
            down = jnp.dot(a_ref[:, :split], wdn_ref[:split, :], preferred_element_type=F32)
    down = down + jnp.dot(a_ref[:, split:], wdn_ref[split:, :], preferred_element_type=F32)
    h2 = h + down

    n3 = _rms(h2, gple_ref[...]).astype(BF16)
    gate = jax.nn.sigmoid(jnp.dot(n3, wpg_ref[...], preferred_element_type=F32))
    e = jnp.dot(p_ref[0].astype(BF16), wple_ref[...], preferred_element_type=F32)
    h3 = h2 + e * gate
    y_ref[0] = _rms(h3, gfin_ref[...])


def _ffn(h, p, g_ffn, wup, conv_w, conv_b, wdn, wple, g_ple, wpg, g_final):
    b, s, d = h.shape
    t = min(FFN_TILE, s)
    per = t // HALO
    nhb = s // HALO
    tile = pl.BlockSpec((1, t, d), lambda bi, i: (bi, i, 0))
    return pl.pallas_call(
        _ffn_kernel,
        grid=(b, s // t),
        in_specs=[
            pl.BlockSpec((1, HALO, d), lambda bi, i: (bi, jnp.maximum(i * per - 1, 0), 0)),
            tile,
            pl.BlockSpec((1, HALO, d), lambda bi, i: (bi, jnp.minimum((i + 1) * per, nhb - 1), 0)),
            pl.BlockSpec((1, t, PLE_DIM), lambda bi, i: (bi, i, 0)),
            _const_spec((1, d)),
            _const_spec(wup.shape),
            _const_spec(conv_w.shape),
            _const_spec(conv_b.shape),
            _const_spec(wdn.shape),
            _const_spec(wple.shape),
            _const_spec((1, d)),
            _const_spec(wpg.shape),
            _const_spec((1, d)),
        ],
        out_specs=tile,
        out_shape=jax.ShapeDtypeStruct((b, s, d), F32),
        scratch_shapes=[pltpu.VMEM((2, t + 2 * HALO, FFN_CHUNK), F32),
                        pltpu.VMEM((2, t + 2 * HALO, FFN_CHUNK), F32),
                        pltpu.VMEM((t, D_FF), BF16)],
        compiler_params=pltpu.CompilerParams(
            dimension_semantics=("arbitrary", "arbitrary"), vmem_limit_bytes=VMEM_LIMIT),
        name="ffn",
    )(h, h, h, p, g_ffn, wup, conv_w, conv_b, wdn, wple, g_ple, wpg, g_final)


def _rope_tables_t(s):
    pos = jnp.arange(s, dtype=jnp.int32)

    def tab(ids, dim, theta):
        inv = theta ** (-jnp.arange(0, dim, 2, dtype=F32) / dim)
        ang = ids.astype(F32)[:, None] * inv[None, :]
        return jnp.cos(ang).T, jnp.sin(ang).T

    ca, sa = tab(pos, A_ROT_DIM, A_ROPE_THETA)
    cr, sr = tab(pos // GRID_W, B_HEAD_DIM // 2, B_AXIAL_THETA)
    cc, sc = tab(pos % GRID_W, B_HEAD_DIM // 2, B_AXIAL_THETA)
    return jnp.concatenate([ca, cr, cc], axis=0), jnp.concatenate([sa, sr, sc], axis=0)


def _encode(x, p, cos_t, sin_t, w):
    qat, ka, vat, qbt, kb, vbt, ga, gb, qna, kna, qnb, knb = _project(
        x, cos_t, sin_t, w["g_mix"], w["wt"], w["wg"], w["g_qn"], w["g_kn"])
    oa = _attention_a(w["lq1"], w["lk1"], w["lq2"], w["lk2"], w["g_diff"], qat, ka, vat, qna, kna)
    ob = _attention_b(qbt, kb, vbt, qnb, knb)
    h1 = _merge(x, oa, ob, ga, gb, w["w_a"], w["w_b"], w["w_out"])
    return _ffn(h1, p, w["g_ffn"], w["w_up"], w["conv_w"], w["conv_b"], w["w_down"],
                w["w_ple"], w["g_ple"], w["w_ple_gate"], w["g_final"])


def kernel(x_prompt, x_sample, p_prompt, p_sample, g_mix, w_in, lambda_q1, lambda_k1, lambda_q2,
           lambda_k2, g_diff, w_a, g_qn, g_kn, w_b, w_out, g_ffn, w_up, conv_w, conv_b, w_down,
           w_ple, g_ple, w_ple_gate, g_final):
    w_in0 = w_in[0]
    row = lambda v: v.reshape(1, -1).astype(F32)
    col = lambda v: v.reshape(-1, 1).astype(F32)
    w = dict(
        g_mix=row(g_mix[0]),
        wt=w_in0[:, :N_TRANSPOSED].T.astype(BF16),
        wg=w_in0[:, N_TRANSPOSED:].astype(BF16),
        g_qn=col(g_qn[0]), g_kn=col(g_kn[0]),
        lq1=row(lambda_q1[0]), lk1=row(lambda_k1[0]), lq2=row(lambda_q2[0]), lk2=row(lambda_k2[0]),
        g_diff=col(g_diff[0]),
        w_a=w_a[0].astype(BF16), w_b=w_b[0].astype(BF16), w_out=w_out[0].astype(BF16),
        g_ffn=row(g_ffn[0]),
        w_up=w_up[0].astype(BF16), conv_w=conv_w[0].astype(F32), conv_b=row(conv_b[0]),
        w_down=w_down[0].astype(BF16),
        w_ple=w_ple[0].astype(BF16), g_ple=row(g_ple[0]), w_ple_gate=w_ple_gate[0].astype(BF16),
        g_final=row(g_final),
    )
    outs = []
    for x, p in ((x_prompt, p_prompt), (x_sample, p_sample)):
        cos_t, sin_t = _rope_tables_t(x.shape[1])
        outs.append(_encode(x, p[0], cos_t, sin_t, w))
    return tuple(outs)
```

```python
import functools
import math

import jax
import jax.numpy as jnp
from jax import lax
from jax.experimental import pallas as pl
from jax.experimental.pallas import tpu as pltpu

F32 = jnp.float32
BF16 = jnp.bfloat16

D_MODEL = 1024
PLE_DIM = 256
GRID_W = 64
EPS = 1e-6
A_HEADS = 8
A_HEAD_DIM = 64
A_ROT_DIM = A_HEAD_DIM // 4
A_ROPE_THETA = 500000.0
B_HEADS = 16
B_KV_HEADS = 4
B_HEAD_DIM = 64
B_REP = B_HEADS // B_KV_HEADS
B_AXIAL_THETA = 10000.0
D_FF = 2816
LAM_INIT = 0.8 - 0.6 * math.exp(-0.3 * 0)

A_W = A_HEADS * 2 * A_HEAD_DIM
B_Q = B_HEADS * B_HEAD_DIM
B_KV = B_KV_HEADS * B_HEAD_DIM
N_TRANSPOSED = 3 * A_W + B_Q + 2 * B_KV

V7X_VMEM_BYTES = 64 * 1024 * 1024
VMEM_LIMIT = 56 * 1024 * 1024
SUBLANES = 8
LANES = 128

TOKEN_TILE = 512
A_Q_TILE = 512
B_Q_TILE = 256
MERGE_TILE = 512
FFN_TILE = 512
FFN_CHUNK = 256
HALO = SUBLANES
SUM_ROWS = 16
A_VROWS = 2 * A_HEAD_DIM + SUM_ROWS
B_VROWS = 2 * B_HEAD_DIM
NEG_BIG = -1e30


def _rms(x, g):
    ms = jnp.mean(x * x, axis=-1, keepdims=True)
    return (x * lax.rsqrt(ms + EPS)) * g


def _rms_rows(x, g):
    ms = jnp.mean(x * x, axis=0, keepdims=True)
    return (x * lax.rsqrt(ms + EPS)) * g


def _proj_kernel(x_ref, cos_ref, sin_ref, gmix_ref, wt_ref, wg_ref, gq_ref, gk_ref,
                 qat_ref, ka_ref, vat_ref, qbt_ref, kb_ref, vbt_ref, ga_ref, gb_ref,
                 qna_ref, kna_ref, qnb_ref, knb_ref):
    x = x_ref[0]
    n = _rms(x, gmix_ref[...])
    nb = n.astype(BF16)
    nt = n.T.astype(BF16)

    cos = cos_ref[...]
    sin = sin_ref[...]
    ca, sa = cos[0:8], sin[0:8]
    cr, sr = cos[8:24], sin[8:24]
    cc, sc = cos[24:40], sin[24:40]

    def rope_a(p):
        x1, x2 = p[0:8], p[8:16]
        return jnp.concatenate([x1 * ca - x2 * sa, x2 * ca + x1 * sa, p[16:64]], axis=0)

    def rope_b(p):
        a1, a2, b1, b2 = p[0:16], p[16:32], p[32:48], p[48:64]
        return jnp.concatenate([a1 * cr - a2 * sr, a2 * cr + a1 * sr,
                                b1 * cc - b2 * sc, b2 * cc + b1 * sc], axis=0)

    def proj_t(lo, hi):
        return jnp.dot(wt_ref[lo:hi, :], nt, preferred_element_type=F32)

    def max_sq_norm(*vecs_t):
        sq = [jnp.max(jnp.sum(v * v, axis=0, keepdims=True)) for v in vecs_t]
        return jnp.full((SUBLANES, LANES), functools.reduce(jnp.maximum, sq), F32)

    scale = A_HEAD_DIM ** -0.5 * math.log2(math.e)


    def put_aq(p):
        for h in range(A_HEADS):
            q1 = rope_a(p[h * 128:h * 128 + 64]) * scale
            q2 = rope_a(p[h * 128 + 64:(h + 1) * 128]) * scale
            qat_ref[0, h * 128:(h + 1) * 128, :] = jnp.concatenate([q1, q2], axis=0).astype(BF16)
            qna_ref[0, h] = max_sq_norm(q1, q2)

    def put_ak(p):
        for s in range(A_W // LANES):
            blk = jnp.concatenate([rope_a(p[s * 128:s * 128 + 64]),
                                   rope_a(p[s * 128 + 64:(s + 1) * 128])], axis=0)
            ka_ref[0, :, s * LANES:(s + 1) * LANES] = blk.T.astype(BF16)
            kna_ref[0, s] = max_sq_norm(blk)

    def put_v(p, v_ref, heads, dv, vrows):
        ones_rows = (lax.broadcasted_iota(jnp.int32, (vrows - dv, nt.shape[1]), 0) == 0).astype(BF16)
        for h in range(heads):
            v_ref[0, 0, h * vrows:h * vrows + dv, :] = p[h * dv:(h + 1) * dv].astype(BF16)
            v_ref[0, 0, h * vrows + dv:(h + 1) * vrows, :] = ones_rows

    def put_bq(p):
        gq = gq_ref[...]
        for g in range(B_KV_HEADS):
            ys = [rope_b(_rms_rows(p[h * 64:(h + 1) * 64], gq)) * scale
                  for h in range(g * B_REP, (g + 1) * B_REP)]
            qbt_ref[0, g * B_REP * 64:(g + 1) * B_REP * 64, :] = jnp.concatenate(ys, axis=0).astype(BF16)
            qnb_ref[0, g] = max_sq_norm(*ys)

    def put_bk(p):
        gk = gk_ref[...]
        zeros = jnp.zeros((LANES - B_HEAD_DIM, p.shape[1]), F32)
        for g in range(B_KV_HEADS):
            y = rope_b(_rms_rows(p[g * 64:(g + 1) * 64], gk))
            kb_ref[0, g] = jnp.concatenate([y, zeros], axis=0).T.astype(BF16)
            knb_ref[0, g] = max_sq_norm(y)

    off_bq = 3 * A_W
    off_bk = off_bq + B_Q
    p_aq = proj_t(0, A_W)
    p_ak = proj_t(A_W, 2 * A_W)
    put_aq(p_aq)
    p_av = proj_t(2 * A_W, off_bq)
    put_ak(p_ak)
    p_bq = proj_t(off_bq, off_bk)
    put_v(p_av, vat_ref, A_HEADS, 2 * A_HEAD_DIM, A_VROWS)
    p_bkv = proj_t(off_bk, off_bk + 2 * B_KV)
    put_bq(p_bq)
    gates = jnp.dot(nb, wg_ref[...], preferred_element_type=F32)
    put_bk(p_bkv[:B_KV])
    put_v(p_bkv[B_KV:], vbt_ref, B_KV_HEADS, B_HEAD_DIM, B_VROWS)
    ga_ref[0] = jax.nn.sigmoid(gates[:, :D_MODEL])
    gb_ref[0] = jax.nn.sigmoid(gates[:, D_MODEL:])


def _const_spec(shape):
    nd = len(shape)
    return pl.BlockSpec(shape, lambda *_: (0,) * nd, pipeline_mode=pl.Buffered(1))


def _project(x, cos_t, sin_t, g_mix, wt, wg, gq, gk):
    b, s, d = x.shape
    t = min(TOKEN_TILE, s)
    nt = s // t
    out_shape = (
        jax.ShapeDtypeStruct((b, A_W, s), BF16),
        jax.ShapeDtypeStruct((b, s, A_W), BF16),
        jax.ShapeDtypeStruct((b, nt, A_HEADS * A_VROWS, t), BF16),
        jax.ShapeDtypeStruct((b, B_Q, s), BF16),
        jax.ShapeDtypeStruct((b, B_KV_HEADS, s, LANES), BF16),
        jax.ShapeDtypeStruct((b, nt, B_KV_HEADS * B_VROWS, t), BF16),
        jax.ShapeDtypeStruct((b, s, D_MODEL), F32),
        jax.ShapeDtypeStruct((b, s, D_MODEL), F32),
        jax.ShapeDtypeStruct((b, A_HEADS, nt * SUBLANES, LANES), F32),
        jax.ShapeDtypeStruct((b, A_HEADS, nt * SUBLANES, LANES), F32),
        jax.ShapeDtypeStruct((b, B_KV_HEADS, nt * SUBLANES, LANES), F32),
        jax.ShapeDtypeStruct((b, B_KV_HEADS, nt * SUBLANES, LANES), F32),
    )
    in_specs = [
        pl.BlockSpec((1, t, d), lambda bi, i: (bi, i, 0)),
        pl.BlockSpec((40, t), lambda bi, i: (0, i)),
        pl.BlockSpec((40, t), lambda bi, i: (0, i)),
        _const_spec((1, d)),
        _const_spec(wt.shape),
        _const_spec(wg.shape),
        _const_spec((B_HEAD_DIM, 1)),
        _const_spec((B_HEAD_DIM, 1)),
    ]
    out_specs = (
        pl.BlockSpec((1, A_W, t), lambda bi, i: (bi, 0, i)),
        pl.BlockSpec((1, t, A_W), lambda bi, i: (bi, i, 0)),
        pl.BlockSpec((1, 1, A_HEADS * A_VROWS, t), lambda bi, i: (bi, i, 0, 0)),
        pl.BlockSpec((1, B_Q, t), lambda bi, i: (bi, 0, i)),
        pl.BlockSpec((1, B_KV_HEADS, t, LANES), lambda bi, i: (bi, 0, i, 0)),
        pl.BlockSpec((1, 1, B_KV_HEADS * B_VROWS, t), lambda bi, i: (bi, i, 0, 0)),
        pl.BlockSpec((1, t, D_MODEL), lambda bi, i: (bi, i, 0)),
        pl.BlockSpec((1, t, D_MODEL), lambda bi, i: (bi, i, 0)),
        pl.BlockSpec((1, A_HEADS, SUBLANES, LANES), lambda bi, i: (bi, 0, i, 0)),
        pl.BlockSpec((1, A_HEADS, SUBLANES, LANES), lambda bi, i: (bi, 0, i, 0)),
        pl.BlockSpec((1, B_KV_HEADS, SUBLANES, LANES), lambda bi, i: (bi, 0, i, 0)),
        pl.BlockSpec((1, B_KV_HEADS, SUBLANES, LANES), lambda bi, i: (bi, 0, i, 0)),
    )
    return pl.pallas_call(
        _proj_kernel,
        grid=(b, nt),
        in_specs=in_specs,
        out_specs=out_specs,
        out_shape=out_shape,
        compiler_params=pltpu.CompilerParams(
            dimension_semantics=("arbitrary", "arbitrary"), vmem_limit_bytes=VMEM_LIMIT),
        name="proj",
    )(x, cos_t, sin_t, g_mix, wt, wg, gq, gk)


KEY_BLOCK = 256


def _flash_t(q_t, k_rows, v_tile_t, s_refs, acc_ref, n_tiles, track_max):
    s_a, s_b = s_refs
    tk = s_a.shape[0]
    acc_ref[...] = jnp.zeros_like(acc_ref)

    def scores(j, s_ref):
        s = jnp.dot(k_rows(pl.multiple_of(j * tk, tk), tk), q_t, preferred_element_type=F32)
        s_ref[...] = s
        return jnp.max(s, axis=0, keepdims=True) if track_max else None

    def consume(v_t, s_ref, m, bmax):
        if not track_max:
            p = jnp.exp2(s_ref[...]).astype(BF16)
            acc_ref[...] += jnp.dot(v_t, p, preferred_element_type=F32)
            return m
        m_new = jnp.maximum(m, bmax)
        alpha = jnp.exp2(m - m_new)
        p = jnp.exp2(s_ref[...] - m_new).astype(BF16)
        acc_ref[...] = alpha * acc_ref[...] + jnp.dot(v_t, p, preferred_element_type=F32)
        return m_new

    def tile(i, carry, last):
        m, bmax_a = carry
        v = v_tile_t(i)
        assert v.shape[1] == 2 * tk
        bmax_b = scores(2 * i + 1, s_b)
        m = consume(v[:, :tk], s_a, m, bmax_a)
        if not last:
            bmax_a = scores(2 * i + 2, s_a)
        m = consume(v[:, tk:], s_b, m, bmax_b)
        return m, bmax_a

    bmax = scores(0, s_a)
    carry = (jnp.full((1, q_t.shape[1]), NEG_BIG, F32), bmax) if track_max else (None, None)
    n_loop = n_tiles - 1
    unroll = next(u for u in ((5, 3, 1) if track_max else (15, 5, 3, 1)) if n_loop % u == 0)
    carry = lax.fori_loop(0, n_loop, lambda i, c: tile(i, c, False), carry, unroll=unroll)
    tile(n_tiles - 1, carry, True)


SCORE_BOUND = 64.0
BOUND_SLACK = 1.05


def _both_paths(qn_ref, kn_ref, q_tiles, n_tiles, body):
    b, h = pl.program_id(0), pl.program_id(1)
    q_sq = functools.reduce(jnp.maximum, [qn_ref[b, h, t] for t in q_tiles])
    k_sq = functools.reduce(jnp.maximum, [kn_ref[b, h, t] for t in range(n_tiles)])
    bounded = q_sq * k_sq * (BOUND_SLACK * BOUND_SLACK) <= SCORE_BOUND * SCORE_BOUND
    pl.when(bounded)(functools.partial(body, False))
    pl.when(jnp.logical_not(bounded))(functools.partial(body, True))


Q_SUBTILES = 2


def _attn_a_kernel(qn_ref, kn_ref, lq1_ref, lk1_ref, lq2_ref, lk2_ref, gd_ref, qt_ref, k_ref, vt_ref,
                   o_ref, s_a, s_b, *acc_refs):
    tq = qt_ref.shape[2] // Q_SUBTILES
    n_tiles, tile = vt_ref.shape[1], vt_ref.shape[3]
    first = pl.program_id(2) * Q_SUBTILES

    def body(track_max):
        for sub, acc_ref in enumerate(acc_refs):
            q = qt_ref[0, :, sub * tq:(sub + 1) * tq]
            row = lax.broadcasted_iota(jnp.int32, q.shape, 0)
            zero = jnp.zeros_like(q)
            q_t = jnp.concatenate([jnp.where(row < A_HEAD_DIM, q, zero),
                                   jnp.where(row >= A_HEAD_DIM, q, zero)], axis=1)
            _flash_t(q_t, lambda r0, n: k_ref[0, pl.ds(r0, n), :], lambda i: vt_ref[0, i],
                     (s_a, s_b), acc_ref, n_tiles, track_max)

            lam = (jnp.exp(jnp.sum(lq1_ref[...] * lk1_ref[...], axis=-1, keepdims=True))
                   - jnp.exp(jnp.sum(lq2_ref[...] * lk2_ref[...], axis=-1, keepdims=True))
                   + LAM_INIT)
            dv = 2 * A_HEAD_DIM
            acc = acc_ref[0:dv, :]
            l = acc_ref[dv:dv + 1, :]
            o = acc[:, :tq] / l[:, :tq] - lam * (acc[:, tq:] / l[:, tq:])
            y = _rms_rows(o, gd_ref[...]) * (1.0 - LAM_INIT)
            o_ref[0, sub * tq:(sub + 1) * tq, :] = y.T.astype(BF16)

    _both_paths(qn_ref, kn_ref, [(first + sub) * tq // tile for sub in range(Q_SUBTILES)], n_tiles, body)


def _attn_b_kernel(qn_ref, kn_ref, qt_ref, k_ref, vt_ref, o_ref, s_a, s_b, *acc_refs):
    tq = qt_ref.shape[2] // Q_SUBTILES
    n_tiles, tile = vt_ref.shape[1], vt_ref.shape[3]
    first = pl.program_id(2) * Q_SUBTILES

    def body(track_max):
        for sub, acc_ref in enumerate(acc_refs):
            q = qt_ref[0, :, sub * tq:(sub + 1) * tq]
            q_cat = jnp.concatenate([q[r * 64:(r + 1) * 64] for r in range(B_REP)], axis=1)
            q_t = jnp.concatenate([q_cat, jnp.zeros_like(q_cat)], axis=0)
            _flash_t(q_t, lambda r0, n: k_ref[0, 0, pl.ds(r0, n), :], lambda i: vt_ref[0, i],
                     (s_a, s_b), acc_ref, n_tiles, track_max)

            o = acc_ref[0:B_HEAD_DIM, :] / acc_ref[B_HEAD_DIM:B_HEAD_DIM + 1, :]
            o_rows = jnp.concatenate([o[:, r * tq:(r + 1) * tq] for r in range(B_REP)], axis=0)
            o_ref[0, sub * tq:(sub + 1) * tq, :] = o_rows.T.astype(BF16)

    _both_paths(qn_ref, kn_ref, [(first + sub) * tq // tile for sub in range(Q_SUBTILES)], n_tiles, body)


def _tile_scalars(stat):
    return stat[:, :, ::SUBLANES, 0]


def _attention_a(lq1, lk1, lq2, lk2, g_diff, qat, ka, vat, qn, kn):
    b, _, s = qat.shape
    tk = vat.shape[3]
    tq = A_Q_TILE
    tstep = Q_SUBTILES * tq
    small = lambda shape: pl.BlockSpec(shape, lambda bi, h, i, *_: (0,) * len(shape))
    return pl.pallas_call(
        _attn_a_kernel,
        grid_spec=pltpu.PrefetchScalarGridSpec(
            num_scalar_prefetch=2,
            grid=(b, A_HEADS, s // tstep),
            in_specs=[
                small((1, A_HEAD_DIM)), small((1, A_HEAD_DIM)), small((1, A_HEAD_DIM)), small((1, A_HEAD_DIM)),
                small((2 * A_HEAD_DIM, 1)),
                pl.BlockSpec((1, 2 * A_HEAD_DIM, tstep), lambda bi, h, i, *_: (bi, h, i)),
                pl.BlockSpec((1, s, 2 * A_HEAD_DIM), lambda bi, h, i, *_: (bi, 0, h)),
                pl.BlockSpec((1, s // tk, A_VROWS, tk), lambda bi, h, i, *_: (bi, 0, h, 0)),
            ],
            out_specs=pl.BlockSpec((1, tstep, 2 * A_HEAD_DIM), lambda bi, h, i, *_: (bi, i, h)),
            scratch_shapes=([pltpu.VMEM((KEY_BLOCK, 2 * tq), F32)] * 2
                            + [pltpu.VMEM((A_VROWS, 2 * tq), F32)] * Q_SUBTILES)),
        out_shape=jax.ShapeDtypeStruct((b, s, A_W), BF16),
        compiler_params=pltpu.CompilerParams(
            dimension_semantics=("arbitrary", "arbitrary", "arbitrary"), vmem_limit_bytes=VMEM_LIMIT),
        name="attn_a",
    )(_tile_scalars(qn), _tile_scalars(kn), lq1, lk1, lq2, lk2, g_diff, qat, ka, vat)


def _attention_b(qbt, kb, vbt, qn, kn):
    b, _, s = qbt.shape
    tk = vbt.shape[3]
    tq = B_Q_TILE
    tstep = Q_SUBTILES * tq
    rows = B_REP * B_HEAD_DIM
    return pl.pallas_call(
        _attn_b_kernel,
        grid_spec=pltpu.PrefetchScalarGridSpec(
            num_scalar_prefetch=2,
            grid=(b, B_KV_HEADS, s // tstep),
            in_specs=[
                pl.BlockSpec((1, rows, tstep), lambda bi, g, i, *_: (bi, g, i)),
                pl.BlockSpec((1, 1, s, LANES), lambda bi, g, i, *_: (bi, g, 0, 0)),
                pl.BlockSpec((1, s // tk, B_VROWS, tk), lambda bi, g, i, *_: (bi, 0, g, 0)),
            ],
            out_specs=pl.BlockSpec((1, tstep, rows), lambda bi, g, i, *_: (bi, i, g)),
            scratch_shapes=([pltpu.VMEM((KEY_BLOCK, B_REP * tq), F32)] * 2
                            + [pltpu.VMEM((B_VROWS, B_REP * tq), F32)] * Q_SUBTILES)),
        out_shape=jax.ShapeDtypeStruct((b, s, B_Q), BF16),
        compiler_params=pltpu.CompilerParams(
            dimension_semantics=("arbitrary", "arbitrary", "arbitrary"), vmem_limit_bytes=VMEM_LIMIT),
        name="attn_b",
    )(_tile_scalars(qn), _tile_scalars(kn), qbt, kb, vbt)


def _merge_kernel(x_ref, oa_ref, ob_ref, ga_ref, gb_ref, wa_ref, wb_ref, wo_ref, h_ref):
    ya = jnp.dot(oa_ref[0], wa_ref[...], preferred_element_type=F32)
    yb = jnp.dot(ob_ref[0], wb_ref[...], preferred_element_type=F32)
    merged = ga_ref[0] * ya + gb_ref[0] * yb
    h_ref[0] = x_ref[0] + jnp.dot(merged.astype(BF16), wo_ref[...], preferred_element_type=F32)


def _merge(x, oa, ob, ga, gb, wa, wb, wo):
    b, s, d = x.shape
    t = min(MERGE_TILE, s)
    tile = pl.BlockSpec((1, t, d), lambda bi, i: (bi, i, 0))
    return pl.pallas_call(
        _merge_kernel,
        grid=(b, s // t),
        in_specs=[tile, tile, tile, tile, tile,
                  _const_spec(wa.shape), _const_spec(wb.shape), _const_spec(wo.shape)],
        out_specs=tile,
        out_shape=jax.ShapeDtypeStruct((b, s, d), F32),
        compiler_params=pltpu.CompilerParams(
            dimension_semantics=("arbitrary", "arbitrary"), vmem_limit_bytes=VMEM_LIMIT),
        name="merge",
    )(x, oa, ob, ga, gb, wa, wb, wo)


def _ffn_kernel(hp_ref, h_ref, hn_ref, p_ref, gffn_ref, wup_ref, cw_ref, cb_ref, wdn_ref,
                wple_ref, gple_ref, wpg_ref, gfin_ref, y_ref, ug_ref, uv_ref, a_ref):
    h = h_ref[0]
    t = h.shape[0]
    rows = t + 2 * HALO
    first_row = pl.program_id(1) * t
    seq_len = pl.num_programs(1) * t
    hh = jnp.concatenate([hp_ref[0], h, hn_ref[0]], axis=0)
    r = lax.broadcasted_iota(jnp.int32, (rows, 1), 0) + (first_row - HALO)
    inside = jnp.logical_and(r >= 0, r < seq_len)
    n2 = jnp.where(inside, _rms(hh, gffn_ref[...]), 0.0).astype(BF16)

    def up_proj(c):
        c0 = c * FFN_CHUNK
        ug_ref[c % 2] = jnp.dot(n2, wup_ref[:, c0:c0 + FFN_CHUNK], preferred_element_type=F32)
        uv_ref[c % 2] = jnp.dot(n2, wup_ref[:, D_FF + c0:D_FF + c0 + FFN_CHUNK],
                                preferred_element_type=F32)

    def conv(u_ref, slot, c0):
        w = cw_ref[:, c0:c0 + FFN_CHUNK]
        um = u_ref[slot, HALO - 1:HALO - 1 + t, :]
        uc = u_ref[slot, HALO:HALO + t, :]
        up = u_ref[slot, HALO + 1:HALO + 1 + t, :]
        return um * w[0:1] + uc * w[1:2] + up * w[2:3] + cb_ref[:, c0:c0 + FFN_CHUNK]

    n_chunks = D_FF // FFN_CHUNK
    split = (n_chunks + 1) // 2 * FFN_CHUNK
    up_proj(0)
    for c in range(n_chunks):
        if c + 1 < n_chunks:
            up_proj(c + 1)
        c0 = c * FFN_CHUNK
        a = jax.nn.gelu(conv(ug_ref, c % 2, c0)) * conv(uv_ref, c % 2, D_FF + c0)
        a_ref[:, c0:c0 + FFN_CHUNK] = a.astype(BF16)
        if c0 + FFN_CHUNK == split:
            down = jnp.dot(a_ref[:, :split], wdn_ref[:split, :], preferred_element_type=F32)
    down = down + jnp.dot(a_ref[:, split:], wdn_ref[split:, :], preferred_element_type=F32)
    h2 = h + down

    n3 = _rms(h2, gple_ref[...]).astype(BF16)
    gate = jax.nn.sigmoid(jnp.dot(n3, wpg_ref[...], preferred_element_type=F32))
    e = jnp.dot(p_ref[0].astype(BF16), wple_ref[...], preferred_element_type=F32)
    h3 = h2 + e * gate
    y_ref[0] = _rms(h3, gfin_ref[...])


def _ffn(h, p, g_ffn, wup, conv_w, conv_b, wdn, wple, g_ple, wpg, g_final):
    b, s, d = h.shape
    t = min(FFN_TILE, s)
    per = t // HALO
    nhb = s // HALO
    tile = pl.BlockSpec((1, t, d), lambda bi, i: (bi, i, 0))
    return pl.pallas_call(
        _ffn_kernel,
        grid=(b, s // t),
        in_specs=[
            pl.BlockSpec((1, HALO, d), lambda bi, i: (bi, jnp.maximum(i * per - 1, 0), 0)),
            tile,
            pl.BlockSpec((1, HALO, d), lambda bi, i: (bi, jnp.minimum((i + 1) * per, nhb - 1), 0)),
            pl.BlockSpec((1, t, PLE_DIM), lambda bi, i: (bi, i, 0)),
            _const_spec((1, d)),
            _const_spec(wup.shape),
            _const_spec(conv_w.shape),
            _const_spec(conv_b.shape),
            _const_spec(wdn.shape),
            _const_spec(wple.shape),
            _const_spec((1, d)),
            _const_spec(wpg.shape),
            _const_spec((1, d)),
        ],
        out_specs=tile,
        out_shape=jax.ShapeDtypeStruct((b, s, d), F32),
        scratch_shapes=[pltpu.VMEM((2, t + 2 * HALO, FFN_CHUNK), F32),
                        pltpu.VMEM((2, t + 2 * HALO, FFN_CHUNK), F32),
                        pltpu.VMEM((t, D_FF), BF16)],
        compiler_params=pltpu.CompilerParams(
            dimension_semantics=("arbitrary", "arbitrary"), vmem_limit_bytes=VMEM_LIMIT),
        name="ffn",
    )(h, h, h, p, g_ffn, wup, conv_w, conv_b, wdn, wple, g_ple, wpg, g_final)


def _rope_tables_t(s):
    pos = jnp.arange(s, dtype=jnp.int32)

    def tab(ids, dim, theta):
        inv = theta ** (-jnp.arange(0, dim, 2, dtype=F32) / dim)
        ang = ids.astype(F32)[:, None] * inv[None, :]
        return jnp.cos(ang).T, jnp.sin(ang).T

    ca, sa = tab(pos, A_ROT_DIM, A_ROPE_THETA)
    cr, sr = tab(pos // GRID_W, B_HEAD_DIM // 2, B_AXIAL_THETA)
    cc, sc = tab(pos % GRID_W, B_HEAD_DIM // 2, B_AXIAL_THETA)
    return jnp.concatenate([ca, cr, cc], axis=0), jnp.concatenate([sa, sr, sc], axis=0)


def _encode(x, p, cos_t, sin_t, w):
    qat, ka, vat, qbt, kb, vbt, ga, gb, qna, kna, qnb, knb = _project(
        x, cos_t, sin_t, w["g_mix"], w["wt"], w["wg"], w["g_qn"], w["g_kn"])
    oa = _attention_a(w["lq1"], w["lk1"], w["lq2"], w["lk2"], w["g_diff"], qat, ka, vat, qna, kna)
    ob = _attention_b(qbt, kb, vbt, qnb, knb)
    h1 = _merge(x, oa, ob, ga, gb, w["w_a"], w["w_b"], w["w_out"])
    return _ffn(h1, p, w["g_ffn"], w["w_up"], w["conv_w"], w["conv_b"], w["w_down"],
                w["w_ple"], w["g_ple"], w["w_ple_gate"], w["g_final"])


def kernel(x_prompt, x_sample, p_prompt, p_sample, g_mix, w_in, lambda_q1, lambda_k1, lambda_q2,
           lambda_k2, g_diff, w_a, g_qn, g_kn, w_b, w_out, g_ffn, w_up, conv_w, conv_b, w_down,
           w_ple, g_ple, w_ple_gate, g_final):
    w_in0 = w_in[0]
    row = lambda v: v.reshape(1, -1).astype(F32)
    col = lambda v: v.reshape(-1, 1).astype(F32)
    w = dict(
        g_mix=row(g_mix[0]),
        wt=w_in0[:, :N_TRANSPOSED].T.astype(BF16),
        wg=w_in0[:, N_TRANSPOSED:].astype(BF16),
        g_qn=col(g_qn[0]), g_kn=col(g_kn[0]),
        lq1=row(lambda_q1[0]), lk1=row(lambda_k1[0]), lq2=row(lambda_q2[0]), lk2=row(lambda_k2[0]),
        g_diff=col(g_diff[0]),
        w_a=w_a[0].astype(BF16), w_b=w_b[0].astype(BF16), w_out=w_out[0].astype(BF16),
        g_ffn=row(g_ffn[0]),
        w_up=w_up[0].astype(BF16), conv_w=conv_w[0].astype(F32), conv_b=row(conv_b[0]),
        w_down=w_down[0].astype(BF16),
        w_ple=w_ple[0].astype(BF16), g_ple=row(g_ple[0]), w_ple_gate=w_ple_gate[0].astype(BF16),
        g_final=row(g_final),
    )
    outs = []
    for x, p in ((x_prompt, p_prompt), (x_sample, p_sample)):
        cos_t, sin_t = _rope_tables_t(x.shape[1])
        outs.append(_encode(x, p[0], cos_t, sin_t, w))
    return tuple(outs)
```

```python
import functools
import math

import jax
import jax.numpy as jnp
from jax import lax
from jax.experimental import pallas as pl
from jax.experimental.pallas import tpu as pltpu

F32 = jnp.float32
BF16 = jnp.bfloat16

D_MODEL = 1024
PLE_DIM = 256
GRID_W = 64
EPS = 1e-6
A_HEADS = 8
A_HEAD_DIM = 64
A_ROT_DIM = A_HEAD_DIM // 4
A_ROPE_THETA = 500000.0
B_HEADS = 16
B_KV_HEADS = 4
B_HEAD_DIM = 64
B_REP = B_HEADS // B_KV_HEADS
B_AXIAL_THETA = 10000.0
D_FF = 2816
LAM_INIT = 0.8 - 0.6 * math.exp(-0.3 * 0)

A_W = A_HEADS * 2 * A_HEAD_DIM
B_Q = B_HEADS * B_HEAD_DIM
B_KV = B_KV_HEADS * B_HEAD_DIM
N_TRANSPOSED = 3 * A_W + B_Q + 2 * B_KV

V7X_VMEM_BYTES = 64 * 1024 * 1024
VMEM_LIMIT = 56 * 1024 * 1024
SUBLANES = 8
LANES = 128

TOKEN_TILE = 512
A_Q_TILE = 512
B_Q_TILE = 256
MERGE_TILE = 1024
FFN_TILE = 512
FFN_CHUNK = 256
HALO = SUBLANES
SUM_ROWS = 16
A_VROWS = 2 * A_HEAD_DIM + SUM_ROWS
B_VROWS = 2 * B_HEAD_DIM
NEG_BIG = -1e30


def _rms(x, g):
    ms = jnp.mean(x * x, axis=-1, keepdims=True)
    return (x * lax.rsqrt(ms + EPS)) * g


def _rms_rows(x, g):
    ms = jnp.mean(x * x, axis=0, keepdims=True)
    return (x * lax.rsqrt(ms + EPS)) * g


def _proj_kernel(x_ref, cos_ref, sin_ref, gmix_ref, wt_ref, wg_ref, gq_ref, gk_ref,
                 qat_ref, ka_ref, vat_ref, qbt_ref, kb_ref, vbt_ref, ga_ref, gb_ref,
                 qna_ref, kna_ref, qnb_ref, knb_ref):
    x = x_ref[0]
    n = _rms(x, gmix_ref[...])
    nb = n.astype(BF16)
    nt = n.T.astype(BF16)

    cos = cos_ref[...]
    sin = sin_ref[...]
    ca, sa = cos[0:8], sin[0:8]
    cr, sr = cos[8:24], sin[8:24]
    cc, sc = cos[24:40], sin[24:40]

    def rope_a(p):
        x1, x2 = p[0:8], p[8:16]
        return jnp.concatenate([x1 * ca - x2 * sa, x2 * ca + x1 * sa, p[16:64]], axis=0)

    def rope_b(p):
        a1, a2, b1, b2 = p[0:16], p[16:32], p[32:48], p[48:64]
        return jnp.concatenate([a1 * cr - a2 * sr, a2 * cr + a1 * sr,
                                b1 * cc - b2 * sc, b2 * cc + b1 * sc], axis=0)

    def proj_t(lo, hi):
        return jnp.dot(wt_ref[lo:hi, :], nt, preferred_element_type=F32)

    def max_sq_norm(*vecs_t):
        sq = [jnp.max(jnp.sum(v * v, axis=0, keepdims=True)) for v in vecs_t]
        return jnp.full((SUBLANES, LANES), functools.reduce(jnp.maximum, sq), F32)

    scale = A_HEAD_DIM ** -0.5 * math.log2(math.e)


    def put_aq(p):
        for h in range(A_HEADS):
            q1 = rope_a(p[h * 128:h * 128 + 64]) * scale
            q2 = rope_a(p[h * 128 + 64:(h + 1) * 128]) * scale
            qat_ref[0, h * 128:(h + 1) * 128, :] = jnp.concatenate([q1, q2], axis=0).astype(BF16)
            qna_ref[0, h] = max_sq_norm(q1, q2)

    def put_ak(p):
        for s in range(A_W // LANES):
            blk = jnp.concatenate([rope_a(p[s * 128:s * 128 + 64]),
                                   rope_a(p[s * 128 + 64:(s + 1) * 128])], axis=0)
            ka_ref[0, :, s * LANES:(s + 1) * LANES] = blk.T.astype(BF16)
            kna_ref[0, s] = max_sq_norm(blk)

    def put_v(p, v_ref, heads, dv, vrows):
        ones_rows = (lax.broadcasted_iota(jnp.int32, (vrows - dv, nt.shape[1]), 0) == 0).astype(BF16)
        for h in range(heads):
            v_ref[0, 0, h * vrows:h * vrows + dv, :] = p[h * dv:(h + 1) * dv].astype(BF16)
            v_ref[0, 0, h * vrows + dv:(h + 1) * vrows, :] = ones_rows

    def put_bq(p):
        gq = gq_ref[...]
        for g in range(B_KV_HEADS):
            ys = [rope_b(_rms_rows(p[h * 64:(h + 1) * 64], gq)) * scale
                  for h in range(g * B_REP, (g + 1) * B_REP)]
            qbt_ref[0, g * B_REP * 64:(g + 1) * B_REP * 64, :] = jnp.concatenate(ys, axis=0).astype(BF16)
            qnb_ref[0, g] = max_sq_norm(*ys)

    def put_bk(p):
        gk = gk_ref[...]
        zeros = jnp.zeros((LANES - B_HEAD_DIM, p.shape[1]), F32)
        for g in range(B_KV_HEADS):
            y = rope_b(_rms_rows(p[g * 64:(g + 1) * 64], gk))
            kb_ref[0, g] = jnp.concatenate([y, zeros], axis=0).T.astype(BF16)
            knb_ref[0, g] = max_sq_norm(y)

    off_bq = 3 * A_W
    off_bk = off_bq + B_Q
    p_aq = proj_t(0, A_W)
    p_ak = proj_t(A_W, 2 * A_W)
    put_aq(p_aq)
    p_av = proj_t(2 * A_W, off_bq)
    put_ak(p_ak)
    p_bq = proj_t(off_bq, off_bk)
    put_v(p_av, vat_ref, A_HEADS, 2 * A_HEAD_DIM, A_VROWS)
    p_bkv = proj_t(off_bk, off_bk + 2 * B_KV)
    put_bq(p_bq)
    gates = jnp.dot(nb, wg_ref[...], preferred_element_type=F32)
    put_bk(p_bkv[:B_KV])
    put_v(p_bkv[B_KV:], vbt_ref, B_KV_HEADS, B_HEAD_DIM, B_VROWS)
    ga_ref[0] = jax.nn.sigmoid(gates[:, :D_MODEL])
    gb_ref[0] = jax.nn.sigmoid(gates[:, D_MODEL:])


def _const_spec(shape):
    nd = len(shape)
    return pl.BlockSpec(shape, lambda *_: (0,) * nd, pipeline_mode=pl.Buffered(1))


def _project(x, cos_t, sin_t, g_mix, wt, wg, gq, gk):
    b, s, d = x.shape
    t = min(TOKEN_TILE, s)
    nt = s // t
    out_shape = (
        jax.ShapeDtypeStruct((b, A_W, s), BF16),
        jax.ShapeDtypeStruct((b, s, A_W), BF16),
        jax.ShapeDtypeStruct((b, nt, A_HEADS * A_VROWS, t), BF16),
        jax.ShapeDtypeStruct((b, B_Q, s), BF16),
        jax.ShapeDtypeStruct((b, B_KV_HEADS, s, LANES), BF16),
        jax.ShapeDtypeStruct((b, nt, B_KV_HEADS * B_VROWS, t), BF16),
        jax.ShapeDtypeStruct((b, s, D_MODEL), F32),
        jax.ShapeDtypeStruct((b, s, D_MODEL), F32),
        jax.ShapeDtypeStruct((b, A_HEADS, nt * SUBLANES, LANES), F32),
        jax.ShapeDtypeStruct((b, A_HEADS, nt * SUBLANES, LANES), F32),
        jax.ShapeDtypeStruct((b, B_KV_HEADS, nt * SUBLANES, LANES), F32),
        jax.ShapeDtypeStruct((b, B_KV_HEADS, nt * SUBLANES, LANES), F32),
    )
    in_specs = [
        pl.BlockSpec((1, t, d), lambda bi, i: (bi, i, 0)),
        pl.BlockSpec((40, t), lambda bi, i: (0, i)),
        pl.BlockSpec((40, t), lambda bi, i: (0, i)),
        _const_spec((1, d)),
        _const_spec(wt.shape),
        _const_spec(wg.shape),
        _const_spec((B_HEAD_DIM, 1)),
        _const_spec((B_HEAD_DIM, 1)),
    ]
    out_specs = (
        pl.BlockSpec((1, A_W, t), lambda bi, i: (bi, 0, i)),
        pl.BlockSpec((1, t, A_W), lambda bi, i: (bi, i, 0)),
        pl.BlockSpec((1, 1, A_HEADS * A_VROWS, t), lambda bi, i: (bi, i, 0, 0)),
        pl.BlockSpec((1, B_Q, t), lambda bi, i: (bi, 0, i)),
        pl.BlockSpec((1, B_KV_HEADS, t, LANES), lambda bi, i: (bi, 0, i, 0)),
        pl.BlockSpec((1, 1, B_KV_HEADS * B_VROWS, t), lambda bi, i: (bi, i, 0, 0)),
        pl.BlockSpec((1, t, D_MODEL), lambda bi, i: (bi, i, 0)),
        pl.BlockSpec((1, t, D_MODEL), lambda bi, i: (bi, i, 0)),
        pl.BlockSpec((1, A_HEADS, SUBLANES, LANES), lambda bi, i: (bi, 0, i, 0)),
        pl.BlockSpec((1, A_HEADS, SUBLANES, LANES), lambda bi, i: (bi, 0, i, 0)),
        pl.BlockSpec((1, B_KV_HEADS, SUBLANES, LANES), lambda bi, i: (bi, 0, i, 0)),
        pl.BlockSpec((1, B_KV_HEADS, SUBLANES, LANES), lambda bi, i: (bi, 0, i, 0)),
    )
    return pl.pallas_call(
        _proj_kernel,
        grid=(b, nt),
        in_specs=in_specs,
        out_specs=out_specs,
        out_shape=out_shape,
        compiler_params=pltpu.CompilerParams(
            dimension_semantics=("arbitrary", "arbitrary"), vmem_limit_bytes=VMEM_LIMIT),
        name="proj",
    )(x, cos_t, sin_t, g_mix, wt, wg, gq, gk)


KEY_BLOCK = 256


def _flash_t(q_t, k_rows, v_tile_t, s_refs, acc_ref, n_tiles, track_max):
    s_a, s_b = s_refs
    tk = s_a.shape[0]
    acc_ref[...] = jnp.zeros_like(acc_ref)

    def scores(j, s_ref):
        s = jnp.dot(k_rows(pl.multiple_of(j * tk, tk), tk), q_t, preferred_element_type=F32)
        s_ref[...] = s
        return jnp.max(s, axis=0, keepdims=True) if track_max else None

    def consume(v_t, s_ref, m, bmax):
        if not track_max:
            p = jnp.exp2(s_ref[...]).astype(BF16)
            acc_ref[...] += jnp.dot(v_t, p, preferred_element_type=F32)
            return m
        m_new = jnp.maximum(m, bmax)
        alpha = jnp.exp2(m - m_new)
        p = jnp.exp2(s_ref[...] - m_new).astype(BF16)
        acc_ref[...] = alpha * acc_ref[...] + jnp.dot(v_t, p, preferred_element_type=F32)
        return m_new

    def tile(i, carry, last):
        m, bmax_a = carry
        v = v_tile_t(i)
        assert v.shape[1] == 2 * tk
        bmax_b = scores(2 * i + 1, s_b)
        m = consume(v[:, :tk], s_a, m, bmax_a)
        if not last:
            bmax_a = scores(2 * i + 2, s_a)
        m = consume(v[:, tk:], s_b, m, bmax_b)
        return m, bmax_a

    bmax = scores(0, s_a)
    carry = (jnp.full((1, q_t.shape[1]), NEG_BIG, F32), bmax) if track_max else (None, None)
    n_loop = n_tiles - 1
    unroll = next(u for u in ((5, 3, 1) if track_max else (15, 5, 3, 1)) if n_loop % u == 0)
    carry = lax.fori_loop(0, n_loop, lambda i, c: tile(i, c, False), carry, unroll=unroll)
    tile(n_tiles - 1, carry, True)


SCORE_BOUND = 64.0
BOUND_SLACK = 1.05


def _both_paths(qn_ref, kn_ref, q_tiles, n_tiles, body):
    b, h = pl.program_id(0), pl.program_id(1)
    q_sq = functools.reduce(jnp.maximum, [qn_ref[b, h, t] for t in q_tiles])
    k_sq = functools.reduce(jnp.maximum, [kn_ref[b, h, t] for t in range(n_tiles)])
    bounded = q_sq * k_sq * (BOUND_SLACK * BOUND_SLACK) <= SCORE_BOUND * SCORE_BOUND
    pl.when(bounded)(functools.partial(body, False))
    pl.when(jnp.logical_not(bounded))(functools.partial(body, True))


Q_SUBTILES = 2


def _attn_a_kernel(qn_ref, kn_ref, lq1_ref, lk1_ref, lq2_ref, lk2_ref, gd_ref, qt_ref, k_ref, vt_ref,
                   o_ref, s_a, s_b, *acc_refs):
    tq = qt_ref.shape[2] // Q_SUBTILES
    n_tiles, tile = vt_ref.shape[1], vt_ref.shape[3]
    first = pl.program_id(2) * Q_SUBTILES

    def body(track_max):
        for sub, acc_ref in enumerate(acc_refs):
            q = qt_ref[0, :, sub * tq:(sub + 1) * tq]
            row = lax.broadcasted_iota(jnp.int32, q.shape, 0)
            zero = jnp.zeros_like(q)
            q_t = jnp.concatenate([jnp.where(row < A_HEAD_DIM, q, zero),
                                   jnp.where(row >= A_HEAD_DIM, q, zero)], axis=1)
            _flash_t(q_t, lambda r0, n: k_ref[0, pl.ds(r0, n), :], lambda i: vt_ref[0, i],
                     (s_a, s_b), acc_ref, n_tiles, track_max)

            lam = (jnp.exp(jnp.sum(lq1_ref[...] * lk1_ref[...], axis=-1, keepdims=True))
                   - jnp.exp(jnp.sum(lq2_ref[...] * lk2_ref[...], axis=-1, keepdims=True))
                   + LAM_INIT)
            dv = 2 * A_HEAD_DIM
            acc = acc_ref[0:dv, :]
            l = acc_ref[dv:dv + 1, :]
            o = acc[:, :tq] / l[:, :tq] - lam * (acc[:, tq:] / l[:, tq:])
            y = _rms_rows(o, gd_ref[...]) * (1.0 - LAM_INIT)
            o_ref[0, sub * tq:(sub + 1) * tq, :] = y.T.astype(BF16)

    _both_paths(qn_ref, kn_ref, [(first + sub) * tq // tile for sub in range(Q_SUBTILES)], n_tiles, body)


def _attn_b_kernel(qn_ref, kn_ref, qt_ref, k_ref, vt_ref, o_ref, s_a, s_b, *acc_refs):
    tq = qt_ref.shape[2] // Q_SUBTILES
    n_tiles, tile = vt_ref.shape[1], vt_ref.shape[3]
    first = pl.program_id(2) * Q_SUBTILES

    def body(track_max):
        for sub, acc_ref in enumerate(acc_refs):
            q = qt_ref[0, :, sub * tq:(sub + 1) * tq]
            q_cat = jnp.concatenate([q[r * 64:(r + 1) * 64] for r in range(B_REP)], axis=1)
            q_t = jnp.concatenate([q_cat, jnp.zeros_like(q_cat)], axis=0)
            _flash_t(q_t, lambda r0, n: k_ref[0, 0, pl.ds(r0, n), :], lambda i: vt_ref[0, i],
                     (s_a, s_b), acc_ref, n_tiles, track_max)

            o = acc_ref[0:B_HEAD_DIM, :] / acc_ref[B_HEAD_DIM:B_HEAD_DIM + 1, :]
            o_rows = jnp.concatenate([o[:, r * tq:(r + 1) * tq] for r in range(B_REP)], axis=0)
            o_ref[0, sub * tq:(sub + 1) * tq, :] = o_rows.T.astype(BF16)

    _both_paths(qn_ref, kn_ref, [(first + sub) * tq // tile for sub in range(Q_SUBTILES)], n_tiles, body)


def _tile_scalars(stat):
    return stat[:, :, ::SUBLANES, 0]


def _attention_a(lq1, lk1, lq2, lk2, g_diff, qat, ka, vat, qn, kn):
    b, _, s = qat.shape
    tk = vat.shape[3]
    tq = A_Q_TILE
    tstep = Q_SUBTILES * tq
    small = lambda shape: pl.BlockSpec(shape, lambda bi, h, i, *_: (0,) * len(shape))
    return pl.pallas_call(
        _attn_a_kernel,
        grid_spec=pltpu.PrefetchScalarGridSpec(
            num_scalar_prefetch=2,
            grid=(b, A_HEADS, s // tstep),
            in_specs=[
                small((1, A_HEAD_DIM)), small((1, A_HEAD_DIM)), small((1, A_HEAD_DIM)), small((1, A_HEAD_DIM)),
                small((2 * A_HEAD_DIM, 1)),
                pl.BlockSpec((1, 2 * A_HEAD_DIM, tstep), lambda bi, h, i, *_: (bi, h, i)),
                pl.BlockSpec((1, s, 2 * A_HEAD_DIM), lambda bi, h, i, *_: (bi, 0, h)),
                pl.BlockSpec((1, s // tk, A_VROWS, tk), lambda bi, h, i, *_: (bi, 0, h, 0)),
            ],
            out_specs=pl.BlockSpec((1, tstep, 2 * A_HEAD_DIM), lambda bi, h, i, *_: (bi, i, h)),
            scratch_shapes=([pltpu.VMEM((KEY_BLOCK, 2 * tq), F32)] * 2
                            + [pltpu.VMEM((A_VROWS, 2 * tq), F32)] * Q_SUBTILES)),
        out_shape=jax.ShapeDtypeStruct((b, s, A_W), BF16),
        compiler_params=pltpu.CompilerParams(
            dimension_semantics=("arbitrary", "arbitrary", "arbitrary"), vmem_limit_bytes=VMEM_LIMIT),
        name="attn_a",
    )(_tile_scalars(qn), _tile_scalars(kn), lq1, lk1, lq2, lk2, g_diff, qat, ka, vat)


def _attention_b(qbt, kb, vbt, qn, kn):
    b, _, s = qbt.shape
    tk = vbt.shape[3]
    tq = B_Q_TILE
    tstep = Q_SUBTILES * tq
    rows = B_REP * B_HEAD_DIM
    return pl.pallas_call(
        _attn_b_kernel,
        grid_spec=pltpu.PrefetchScalarGridSpec(
            num_scalar_prefetch=2,
            grid=(b, B_KV_HEADS, s // tstep),
            in_specs=[
                pl.BlockSpec((1, rows, tstep), lambda bi, g, i, *_: (bi, g, i)),
                pl.BlockSpec((1, 1, s, LANES), lambda bi, g, i, *_: (bi, g, 0, 0)),
                pl.BlockSpec((1, s // tk, B_VROWS, tk), lambda bi, g, i, *_: (bi, 0, g, 0)),
            ],
            out_specs=pl.BlockSpec((1, tstep, rows), lambda bi, g, i, *_: (bi, i, g)),
            scratch_shapes=([pltpu.VMEM((KEY_BLOCK, B_REP * tq), F32)] * 2
                            + [pltpu.VMEM((B_VROWS, B_REP * tq), F32)] * Q_SUBTILES)),
        out_shape=jax.ShapeDtypeStruct((b, s, B_Q), BF16),
        compiler_params=pltpu.CompilerParams(
            dimension_semantics=("arbitrary", "arbitrary", "arbitrary"), vmem_limit_bytes=VMEM_LIMIT),
        name="attn_b",
    )(_tile_scalars(qn), _tile_scalars(kn), qbt, kb, vbt)


def _merge_kernel(x_ref, oa_ref, ob_ref, ga_ref, gb_ref, wa_ref, wb_ref, wo_ref, h_ref):
    ya = jnp.dot(oa_ref[0], wa_ref[...], preferred_element_type=F32)
    yb = jnp.dot(ob_ref[0], wb_ref[...], preferred_element_type=F32)
    merged = ga_ref[0] * ya + gb_ref[0] * yb
    h_ref[0] = x_ref[0] + jnp.dot(merged.astype(BF16), wo_ref[...], preferred_element_type=F32)


def _merge(x, oa, ob, ga, gb, wa, wb, wo):
    b, s, d = x.shape
    t = min(MERGE_TILE, s)
    tile = pl.BlockSpec((1, t, d), lambda bi, i: (bi, i, 0))
    return pl.pallas_call(
        _merge_kernel,
        grid=(b, s // t),
        in_specs=[tile, tile, tile, tile, tile,
                  _const_spec(wa.shape), _const_spec(wb.shape), _const_spec(wo.shape)],
        out_specs=tile,
        out_shape=jax.ShapeDtypeStruct((b, s, d), F32),
        compiler_params=pltpu.CompilerParams(
            dimension_semantics=("arbitrary", "arbitrary"), vmem_limit_bytes=VMEM_LIMIT),
        name="merge",
    )(x, oa, ob, ga, gb, wa, wb, wo)


def _ffn_kernel(hp_ref, h_ref, hn_ref, p_ref, gffn_ref, wup_ref, cw_ref, cb_ref, wdn_ref,
                wple_ref, gple_ref, wpg_ref, gfin_ref, y_ref, ug_ref, uv_ref, a_ref):
    h = h_ref[0]
    t = h.shape[0]
    rows = t + 2 * HALO
    first_row = pl.program_id(1) * t
    seq_len = pl.num_programs(1) * t
    hh = jnp.concatenate([hp_ref[0], h, hn_ref[0]], axis=0)
    r = lax.broadcasted_iota(jnp.int32, (rows, 1), 0) + (first_row - HALO)
    inside = jnp.logical_and(r >= 0, r < seq_len)
    n2 = jnp.where(inside, _rms(hh, gffn_ref[...]), 0.0).astype(BF16)

    def up_proj(c):
        c0 = c * FFN_CHUNK
        ug_ref[c % 2] = jnp.dot(n2, wup_ref[:, c0:c0 + FFN_CHUNK], preferred_element_type=F32)
        uv_ref[c % 2] = jnp.dot(n2, wup_ref[:, D_FF + c0:D_FF + c0 + FFN_CHUNK],
                                preferred_element_type=F32)

    def conv(u_ref, slot, c0):
        w = cw_ref[:, c0:c0 + FFN_CHUNK]
        um = u_ref[slot, HALO - 1:HALO - 1 + t, :]
        uc = u_ref[slot, HALO:HALO + t, :]
        up = u_ref[slot, HALO + 1:HALO + 1 + t, :]
        return um * w[0:1] + uc * w[1:2] + up * w[2:3] + cb_ref[:, c0:c0 + FFN_CHUNK]

    n_chunks = D_FF // FFN_CHUNK
    split = (n_chunks + 1) // 2 * FFN_CHUNK
    up_proj(0)
    for c in range(n_chunks):
        if c + 1 < n_chunks:
            up_proj(c + 1)
        c0 = c * FFN_CHUNK
        a = jax.nn.gelu(conv(ug_ref, c % 2, c0)) * conv(uv_ref, c % 2, D_FF + c0)
        a_ref[:, c0:c0 + FFN_CHUNK] = a.astype(BF16)
        if c0 + FFN_CHUNK == split:
            down = jnp.dot(a_ref[:, :split], wdn_ref[:split, :], preferred_element_type=F32)
    down = down + jnp.dot(a_ref[:, split:], wdn_ref[split:, :], preferred_element_type=F32)
    h2 = h + down

    n3 = _rms(h2, gple_ref[...]).astype(BF16)
    gate = jax.nn.sigmoid(jnp.dot(n3, wpg_ref[...], preferred_element_type=F32))
    e = jnp.dot(p_ref[0].astype(BF16), wple_ref[...], preferred_element_type=F32)
    h3 = h2 + e * gate
    y_ref[0] = _rms(h3, gfin_ref[...])


def _ffn(h, p, g_ffn, wup, conv_w, conv_b, wdn, wple, g_ple, wpg, g_final):
    b, s, d = h.shape
    t = min(FFN_TILE, s)
    per = t // HALO
    nhb = s // HALO
    tile = pl.BlockSpec((1, t, d), lambda bi, i: (bi, i, 0))
    return pl.pallas_call(
        _ffn_kernel,
        grid=(b, s // t),
        in_specs=[
            pl.BlockSpec((1, HALO, d), lambda bi, i: (bi, jnp.maximum(i * per - 1, 0), 0)),
            tile,
            pl.BlockSpec((1, HALO, d), lambda bi, i: (bi, jnp.minimum((i + 1) * per, nhb - 1), 0)),
            pl.BlockSpec((1, t, PLE_DIM), lambda bi, i: (bi, i, 0)),
            _const_spec((1, d)),
            _const_spec(wup.shape),
            _const_spec(conv_w.shape),
            _const_spec(conv_b.shape),
            _const_spec(wdn.shape),
            _const_spec(wple.shape),
            _const_spec((1, d)),
            _const_spec(wpg.shape),
            _const_spec((1, d)),
        ],
        out_specs=tile,
        out_shape=jax.ShapeDtypeStruct((b, s, d), F32),
        scratch_shapes=[pltpu.VMEM((2, t + 2 * HALO, FFN_CHUNK), F32),
                        pltpu.VMEM((2, t + 2 * HALO, FFN_CHUNK), F32),
                        pltpu.VMEM((t, D_FF), BF16)],
        compiler_params=pltpu.CompilerParams(
            dimension_semantics=("arbitrary", "arbitrary"), vmem_limit_bytes=VMEM_LIMIT),
        name="ffn",
    )(h, h, h, p, g_ffn, wup, conv_w, conv_b, wdn, wple, g_ple, wpg, g_final)


def _rope_tables_t(s):
    pos = jnp.arange(s, dtype=jnp.int32)

    def tab(ids, dim, theta):
        inv = theta ** (-jnp.arange(0, dim, 2, dtype=F32) / dim)
        ang = ids.astype(F32)[:, None] * inv[None, :]
        return jnp.cos(ang).T, jnp.sin(ang).T

    ca, sa = tab(pos, A_ROT_DIM, A_ROPE_THETA)
    cr, sr = tab(pos // GRID_W, B_HEAD_DIM // 2, B_AXIAL_THETA)
    cc, sc = tab(pos % GRID_W, B_HEAD_DIM // 2, B_AXIAL_THETA)
    return jnp.concatenate([ca, cr, cc], axis=0), jnp.concatenate([sa, sr, sc], axis=0)


def _encode(x, p, cos_t, sin_t, w):
    qat, ka, vat, qbt, kb, vbt, ga, gb, qna, kna, qnb, knb = _project(
        x, cos_t, sin_t, w["g_mix"], w["wt"], w["wg"], w["g_qn"], w["g_kn"])
    oa = _attention_a(w["lq1"], w["lk1"], w["lq2"], w["lk2"], w["g_diff"], qat, ka, vat, qna, kna)
    ob = _attention_b(qbt, kb, vbt, qnb, knb)
    h1 = _merge(x, oa, ob, ga, gb, w["w_a"], w["w_b"], w["w_out"])
    return _ffn(h1, p, w["g_ffn"], w["w_up"], w["conv_w"], w["conv_b"], w["w_down"],
                w["w_ple"], w["g_ple"], w["w_ple_gate"], w["g_final"])


def kernel(x_prompt, x_sample, p_prompt, p_sample, g_mix, w_in, lambda_q1, lambda_k1, lambda_q2,
           lambda_k2, g_diff, w_a, g_qn, g_kn, w_b, w_out, g_ffn, w_up, conv_w, conv_b, w_down,
           w_ple, g_ple, w_ple_gate, g_final):
    w_in0 = w_in[0]
    row = lambda v: v.reshape(1, -1).astype(F32)
    col = lambda v: v.reshape(-1, 1).astype(F32)
    w = dict(
        g_mix=row(g_mix[0]),
        wt=w_in0[:, :N_TRANSPOSED].T.astype(BF16),
        wg=w_in0[:, N_TRANSPOSED:].astype(BF16),
        g_qn=col(g_qn[0]), g_kn=col(g_kn[0]),
        lq1=row(lambda_q1[0]), lk1=row(lambda_k1[0]), lq2=row(lambda_q2[0]), lk2=row(lambda_k2[0]),
        g_diff=col(g_diff[0]),
        w_a=w_a[0].astype(BF16), w_b=w_b[0].astype(BF16), w_out=w_out[0].astype(BF16),
        g_ffn=row(g_ffn[0]),
        w_up=w_up[0].astype(BF16), conv_w=conv_w[0].astype(F32), conv_b=row(conv_b[0]),
        w_down=w_down[0].astype(BF16),
        w_ple=w_ple[0].astype(BF16), g_ple=row(g_ple[0]), w_ple_gate=w_ple_gate[0].astype(BF16),
        g_final=row(g_final),
    )
    outs = []
    for x, p in ((x_prompt, p_prompt), (x_sample, p_sample)):
        cos_t, sin_t = _rope_tables_t(x.shape[1])
        outs.append(_encode(x, p[0], cos_t, sin_t, w))
    return tuple(outs)
```

```python
import functools
import math

import jax
import jax.numpy as jnp
from jax import lax
from jax.experimental import pallas as pl
from jax.experimental.pallas import tpu as pltpu

F32 = jnp.float32
BF16 = jnp.bfloat16

D_MODEL = 1024
PLE_DIM = 256
GRID_W = 64
EPS = 1e-6
A_HEADS = 8
A_HEAD_DIM = 64
A_ROT_DIM = A_HEAD_DIM // 4
A_ROPE_THETA = 500000.0
B_HEADS = 16
B_KV_HEADS = 4
B_HEAD_DIM = 64
B_REP = B_HEADS // B_KV_HEADS
B_AXIAL_THETA = 10000.0
D_FF = 2816
LAM_INIT = 0.8 - 0.6 * math.exp(-0.3 * 0)

A_W = A_HEADS * 2 * A_HEAD_DIM
B_Q = B_HEADS * B_HEAD_DIM
B_KV = B_KV_HEADS * B_HEAD_DIM
N_TRANSPOSED = 3 * A_W + B_Q + 2 * B_KV

V7X_VMEM_BYTES = 64 * 1024 * 1024
VMEM_LIMIT = V7X_VMEM_BYTES * 7 // 8
SUBLANES = 8
LANES = 128

TOKEN_TILE = 512
A_Q_TILE = 512
B_Q_TILE = 256
MERGE_TILE = 1024
FFN_TILE = 512
FFN_CHUNK = 256
HALO = SUBLANES
SUM_ROWS = 16
A_VROWS = 2 * A_HEAD_DIM + SUM_ROWS
B_VROWS = 2 * B_HEAD_DIM
NEG_BIG = -1e30


def _rms(x, g):
    ms = jnp.mean(x * x, axis=-1, keepdims=True)
    return (x * lax.rsqrt(ms + EPS)) * g


def _rms_rows(x, g):
    ms = jnp.mean(x * x, axis=0, keepdims=True)
    return (x * lax.rsqrt(ms + EPS)) * g


def _proj_kernel(x_ref, cos_ref, sin_ref, gmix_ref, wt_ref, wg_ref, gq_ref, gk_ref,
                 qat_ref, ka_ref, vat_ref, qbt_ref, kb_ref, vbt_ref, ga_ref, gb_ref,
                 qna_ref, kna_ref, qnb_ref, knb_ref):
    x = x_ref[0]
    n = _rms(x, gmix_ref[...])
    nb = n.astype(BF16)
    nt = n.T.astype(BF16)

    cos = cos_ref[...]
    sin = sin_ref[...]
    ca, sa = cos[0:8], sin[0:8]
    cr, sr = cos[8:24], sin[8:24]
    cc, sc = cos[24:40], sin[24:40]

    def rope_a(p):
        x1, x2 = p[0:8], p[8:16]
        return jnp.concatenate([x1 * ca - x2 * sa, x2 * ca + x1 * sa, p[16:64]], axis=0)

    def rope_b(p):
        a1, a2, b1, b2 = p[0:16], p[16:32], p[32:48], p[48:64]
        return jnp.concatenate([a1 * cr - a2 * sr, a2 * cr + a1 * sr,
                                b1 * cc - b2 * sc, b2 * cc + b1 * sc], axis=0)

    def proj_t(lo, hi):
        return jnp.dot(wt_ref[lo:hi, :], nt, preferred_element_type=F32)

    def max_sq_norm(*vecs_t):
        sq = [jnp.max(jnp.sum(v * v, axis=0, keepdims=True)) for v in vecs_t]
        return jnp.full((SUBLANES, LANES), functools.reduce(jnp.maximum, sq), F32)

    scale = A_HEAD_DIM ** -0.5 * math.log2(math.e)


    def put_aq(p):
        for h in range(A_HEADS):
            q1 = rope_a(p[h * 128:h * 128 + 64]) * scale
            q2 = rope_a(p[h * 128 + 64:(h + 1) * 128]) * scale
            qat_ref[0, h * 128:(h + 1) * 128, :] = jnp.concatenate([q1, q2], axis=0).astype(BF16)
            qna_ref[0, h] = max_sq_norm(q1, q2)

    def put_ak(p):
        for s in range(A_W // LANES):
            blk = jnp.concatenate([rope_a(p[s * 128:s * 128 + 64]),
                                   rope_a(p[s * 128 + 64:(s + 1) * 128])], axis=0)
            ka_ref[0, :, s * LANES:(s + 1) * LANES] = blk.T.astype(BF16)
            kna_ref[0, s] = max_sq_norm(blk)

    def put_v(p, v_ref, heads, dv, vrows):
        ones_rows = (lax.broadcasted_iota(jnp.int32, (vrows - dv, nt.shape[1]), 0) == 0).astype(BF16)
        for h in range(heads):
            v_ref[0, 0, h * vrows:h * vrows + dv, :] = p[h * dv:(h + 1) * dv].astype(BF16)
            v_ref[0, 0, h * vrows + dv:(h + 1) * vrows, :] = ones_rows

    def put_bq(p):
        gq = gq_ref[...]
        for g in range(B_KV_HEADS):
            ys = [rope_b(_rms_rows(p[h * 64:(h + 1) * 64], gq)) * scale
                  for h in range(g * B_REP, (g + 1) * B_REP)]
            qbt_ref[0, g * B_REP * 64:(g + 1) * B_REP * 64, :] = jnp.concatenate(ys, axis=0).astype(BF16)
            qnb_ref[0, g] = max_sq_norm(*ys)

    def put_bk(p):
        gk = gk_ref[...]
        zeros = jnp.zeros((LANES - B_HEAD_DIM, p.shape[1]), F32)
        for g in range(B_KV_HEADS):
            y = rope_b(_rms_rows(p[g * 64:(g + 1) * 64], gk))
            kb_ref[0, g] = jnp.concatenate([y, zeros], axis=0).T.astype(BF16)
            knb_ref[0, g] = max_sq_norm(y)

    off_bq = 3 * A_W
    off_bk = off_bq + B_Q
    p_aq = proj_t(0, A_W)
    p_ak = proj_t(A_W, 2 * A_W)
    put_aq(p_aq)
    p_av = proj_t(2 * A_W, off_bq)
    put_ak(p_ak)
    p_bq = proj_t(off_bq, off_bk)
    put_v(p_av, vat_ref, A_HEADS, 2 * A_HEAD_DIM, A_VROWS)
    p_bkv = proj_t(off_bk, off_bk + 2 * B_KV)
    put_bq(p_bq)
    gates = jnp.dot(nb, wg_ref[...], preferred_element_type=F32)
    put_bk(p_bkv[:B_KV])
    put_v(p_bkv[B_KV:], vbt_ref, B_KV_HEADS, B_HEAD_DIM, B_VROWS)
    ga_ref[0] = jax.nn.sigmoid(gates[:, :D_MODEL])
    gb_ref[0] = jax.nn.sigmoid(gates[:, D_MODEL:])


def _const_spec(shape):
    nd = len(shape)
    return pl.BlockSpec(shape, lambda *_: (0,) * nd, pipeline_mode=pl.Buffered(1))


def _project(x, cos_t, sin_t, g_mix, wt, wg, gq, gk):
    b, s, d = x.shape
    t = min(TOKEN_TILE, s)
    nt = s // t
    out_shape = (
        jax.ShapeDtypeStruct((b, A_W, s), BF16),
        jax.ShapeDtypeStruct((b, s, A_W), BF16),
        jax.ShapeDtypeStruct((b, nt, A_HEADS * A_VROWS, t), BF16),
        jax.ShapeDtypeStruct((b, B_Q, s), BF16),
        jax.ShapeDtypeStruct((b, B_KV_HEADS, s, LANES), BF16),
        jax.ShapeDtypeStruct((b, nt, B_KV_HEADS * B_VROWS, t), BF16),
        jax.ShapeDtypeStruct((b, s, D_MODEL), F32),
        jax.ShapeDtypeStruct((b, s, D_MODEL), F32),
        jax.ShapeDtypeStruct((b, A_HEADS, nt * SUBLANES, LANES), F32),
        jax.ShapeDtypeStruct((b, A_HEADS, nt * SUBLANES, LANES), F32),
        jax.ShapeDtypeStruct((b, B_KV_HEADS, nt * SUBLANES, LANES), F32),
        jax.ShapeDtypeStruct((b, B_KV_HEADS, nt * SUBLANES, LANES), F32),
    )
    in_specs = [
        pl.BlockSpec((1, t, d), lambda bi, i: (bi, i, 0)),
        pl.BlockSpec((40, t), lambda bi, i: (0, i)),
        pl.BlockSpec((40, t), lambda bi, i: (0, i)),
        _const_spec((1, d)),
        _const_spec(wt.shape),
        _const_spec(wg.shape),
        _const_spec((B_HEAD_DIM, 1)),
        _const_spec((B_HEAD_DIM, 1)),
    ]
    out_specs = (
        pl.BlockSpec((1, A_W, t), lambda bi, i: (bi, 0, i)),
        pl.BlockSpec((1, t, A_W), lambda bi, i: (bi, i, 0)),
        pl.BlockSpec((1, 1, A_HEADS * A_VROWS, t), lambda bi, i: (bi, i, 0, 0)),
        pl.BlockSpec((1, B_Q, t), lambda bi, i: (bi, 0, i)),
        pl.BlockSpec((1, B_KV_HEADS, t, LANES), lambda bi, i: (bi, 0, i, 0)),
        pl.BlockSpec((1, 1, B_KV_HEADS * B_VROWS, t), lambda bi, i: (bi, i, 0, 0)),
        pl.BlockSpec((1, t, D_MODEL), lambda bi, i: (bi, i, 0)),
        pl.BlockSpec((1, t, D_MODEL), lambda bi, i: (bi, i, 0)),
        pl.BlockSpec((1, A_HEADS, SUBLANES, LANES), lambda bi, i: (bi, 0, i, 0)),
        pl.BlockSpec((1, A_HEADS, SUBLANES, LANES), lambda bi, i: (bi, 0, i, 0)),
        pl.BlockSpec((1, B_KV_HEADS, SUBLANES, LANES), lambda bi, i: (bi, 0, i, 0)),
        pl.BlockSpec((1, B_KV_HEADS, SUBLANES, LANES), lambda bi, i: (bi, 0, i, 0)),
    )
    return pl.pallas_call(
        _proj_kernel,
        grid=(b, nt),
        in_specs=in_specs,
        out_specs=out_specs,
        out_shape=out_shape,
        compiler_params=pltpu.CompilerParams(
            dimension_semantics=("arbitrary", "arbitrary"), vmem_limit_bytes=VMEM_LIMIT),
        name="proj",
    )(x, cos_t, sin_t, g_mix, wt, wg, gq, gk)


KEY_BLOCK = 256


def _flash_t(q_t, k_rows, v_tile_t, s_refs, acc_ref, n_tiles, track_max):
    s_a, s_b = s_refs
    tk = s_a.shape[0]
    acc_ref[...] = jnp.zeros_like(acc_ref)

    def scores(j, s_ref):
        s = jnp.dot(k_rows(pl.multiple_of(j * tk, tk), tk), q_t, preferred_element_type=F32)
        s_ref[...] = s
        return jnp.max(s, axis=0, keepdims=True) if track_max else None

    def consume(v_t, s_ref, m, bmax):
        if not track_max:
            p = jnp.exp2(s_ref[...]).astype(BF16)
            acc_ref[...] += jnp.dot(v_t, p, preferred_element_type=F32)
            return m
        m_new = jnp.maximum(m, bmax)
        alpha = jnp.exp2(m - m_new)
        p = jnp.exp2(s_ref[...] - m_new).astype(BF16)
        acc_ref[...] = alpha * acc_ref[...] + jnp.dot(v_t, p, preferred_element_type=F32)
        return m_new

    def tile(i, carry, last):
        m, bmax_a = carry
        v = v_tile_t(i)
        assert v.shape[1] == 2 * tk
        bmax_b = scores(2 * i + 1, s_b)
        m = consume(v[:, :tk], s_a, m, bmax_a)
        if not last:
            bmax_a = scores(2 * i + 2, s_a)
        m = consume(v[:, tk:], s_b, m, bmax_b)
        return m, bmax_a

    bmax = scores(0, s_a)
    carry = (jnp.full((1, q_t.shape[1]), NEG_BIG, F32), bmax) if track_max else (None, None)
    n_loop = n_tiles - 1
    unroll = next(u for u in ((5, 3, 1) if track_max else (15, 5, 3, 1)) if n_loop % u == 0)
    carry = lax.fori_loop(0, n_loop, lambda i, c: tile(i, c, False), carry, unroll=unroll)
    tile(n_tiles - 1, carry, True)


SCORE_BOUND = 64.0
BOUND_SLACK = 1.05


def _both_paths(qn_ref, kn_ref, q_tiles, n_tiles, body):
    b, h = pl.program_id(0), pl.program_id(1)
    q_sq = functools.reduce(jnp.maximum, [qn_ref[b, h, t] for t in q_tiles])
    k_sq = functools.reduce(jnp.maximum, [kn_ref[b, h, t] for t in range(n_tiles)])
    bounded = q_sq * k_sq * (BOUND_SLACK * BOUND_SLACK) <= SCORE_BOUND * SCORE_BOUND
    pl.when(bounded)(functools.partial(body, False))
    pl.when(jnp.logical_not(bounded))(functools.partial(body, True))


Q_SUBTILES = 2


def _attn_a_kernel(qn_ref, kn_ref, lq1_ref, lk1_ref, lq2_ref, lk2_ref, gd_ref, qt_ref, k_ref, vt_ref,
                   o_ref, s_a, s_b, *acc_refs):
    tq = qt_ref.shape[2] // Q_SUBTILES
    n_tiles, tile = vt_ref.shape[1], vt_ref.shape[3]
    first = pl.program_id(2) * Q_SUBTILES

    def body(track_max):
        for sub, acc_ref in enumerate(acc_refs):
            q = qt_ref[0, :, sub * tq:(sub + 1) * tq]
            row = lax.broadcasted_iota(jnp.int32, q.shape, 0)
            zero = jnp.zeros_like(q)
            q_t = jnp.concatenate([jnp.where(row < A_HEAD_DIM, q, zero),
                                   jnp.where(row >= A_HEAD_DIM, q, zero)], axis=1)
            _flash_t(q_t, lambda r0, n: k_ref[0, pl.ds(r0, n), :], lambda i: vt_ref[0, i],
                     (s_a, s_b), acc_ref, n_tiles, track_max)

            lam = (jnp.exp(jnp.sum(lq1_ref[...] * lk1_ref[...], axis=-1, keepdims=True))
                   - jnp.exp(jnp.sum(lq2_ref[...] * lk2_ref[...], axis=-1, keepdims=True))
                   + LAM_INIT)
            dv = 2 * A_HEAD_DIM
            acc = acc_ref[0:dv, :]
            l = acc_ref[dv:dv + 1, :]
            o = acc[:, :tq] / l[:, :tq] - lam * (acc[:, tq:] / l[:, tq:])
            y = _rms_rows(o, gd_ref[...]) * (1.0 - LAM_INIT)
            o_ref[0, sub * tq:(sub + 1) * tq, :] = y.T.astype(BF16)

    _both_paths(qn_ref, kn_ref, [(first + sub) * tq // tile for sub in range(Q_SUBTILES)], n_tiles, body)


def _attn_b_kernel(qn_ref, kn_ref, qt_ref, k_ref, vt_ref, o_ref, s_a, s_b, *acc_refs):
    tq = qt_ref.shape[2] // Q_SUBTILES
    n_tiles, tile = vt_ref.shape[1], vt_ref.shape[3]
    first = pl.program_id(2) * Q_SUBTILES

    def body(track_max):
        for sub, acc_ref in enumerate(acc_refs):
            q = qt_ref[0, :, sub * tq:(sub + 1) * tq]
            q_cat = jnp.concatenate([q[r * 64:(r + 1) * 64] for r in range(B_REP)], axis=1)
            q_t = jnp.concatenate([q_cat, jnp.zeros_like(q_cat)], axis=0)
            _flash_t(q_t, lambda r0, n: k_ref[0, 0, pl.ds(r0, n), :], lambda i: vt_ref[0, i],
                     (s_a, s_b), acc_ref, n_tiles, track_max)

            o = acc_ref[0:B_HEAD_DIM, :] / acc_ref[B_HEAD_DIM:B_HEAD_DIM + 1, :]
            o_rows = jnp.concatenate([o[:, r * tq:(r + 1) * tq] for r in range(B_REP)], axis=0)
            o_ref[0, sub * tq:(sub + 1) * tq, :] = o_rows.T.astype(BF16)

    _both_paths(qn_ref, kn_ref, [(first + sub) * tq // tile for sub in range(Q_SUBTILES)], n_tiles, body)


def _tile_scalars(stat):
    return stat[:, :, ::SUBLANES, 0]


def _attention_a(lq1, lk1, lq2, lk2, g_diff, qat, ka, vat, qn, kn):
    b, _, s = qat.shape
    tk = vat.shape[3]
    tq = A_Q_TILE
    tstep = Q_SUBTILES * tq
    small = lambda shape: pl.BlockSpec(shape, lambda bi, h, i, *_: (0,) * len(shape))
    return pl.pallas_call(
        _attn_a_kernel,
        grid_spec=pltpu.PrefetchScalarGridSpec(
            num_scalar_prefetch=2,
            grid=(b, A_HEADS, s // tstep),
            in_specs=[
                small((1, A_HEAD_DIM)), small((1, A_HEAD_DIM)), small((1, A_HEAD_DIM)), small((1, A_HEAD_DIM)),
                small((2 * A_HEAD_DIM, 1)),
                pl.BlockSpec((1, 2 * A_HEAD_DIM, tstep), lambda bi, h, i, *_: (bi, h, i)),
                pl.BlockSpec((1, s, 2 * A_HEAD_DIM), lambda bi, h, i, *_: (bi, 0, h)),
                pl.BlockSpec((1, s // tk, A_VROWS, tk), lambda bi, h, i, *_: (bi, 0, h, 0)),
            ],
            out_specs=pl.BlockSpec((1, tstep, 2 * A_HEAD_DIM), lambda bi, h, i, *_: (bi, i, h)),
            scratch_shapes=([pltpu.VMEM((KEY_BLOCK, 2 * tq), F32)] * 2
                            + [pltpu.VMEM((A_VROWS, 2 * tq), F32)] * Q_SUBTILES)),
        out_shape=jax.ShapeDtypeStruct((b, s, A_W), BF16),
        compiler_params=pltpu.CompilerParams(
            dimension_semantics=("arbitrary", "arbitrary", "arbitrary"), vmem_limit_bytes=VMEM_LIMIT),
        name="attn_a",
    )(_tile_scalars(qn), _tile_scalars(kn), lq1, lk1, lq2, lk2, g_diff, qat, ka, vat)


def _attention_b(qbt, kb, vbt, qn, kn):
    b, _, s = qbt.shape
    tk = vbt.shape[3]
    tq = B_Q_TILE
    tstep = Q_SUBTILES * tq
    rows = B_REP * B_HEAD_DIM
    return pl.pallas_call(
        _attn_b_kernel,
        grid_spec=pltpu.PrefetchScalarGridSpec(
            num_scalar_prefetch=2,
            grid=(b, B_KV_HEADS, s // tstep),
            in_specs=[
                pl.BlockSpec((1, rows, tstep), lambda bi, g, i, *_: (bi, g, i)),
                pl.BlockSpec((1, 1, s, LANES), lambda bi, g, i, *_: (bi, g, 0, 0)),
                pl.BlockSpec((1, s // tk, B_VROWS, tk), lambda bi, g, i, *_: (bi, 0, g, 0)),
            ],
            out_specs=pl.BlockSpec((1, tstep, rows), lambda bi, g, i, *_: (bi, i, g)),
            scratch_shapes=([pltpu.VMEM((KEY_BLOCK, B_REP * tq), F32)] * 2
                            + [pltpu.VMEM((B_VROWS, B_REP * tq), F32)] * Q_SUBTILES)),
        out_shape=jax.ShapeDtypeStruct((b, s, B_Q), BF16),
        compiler_params=pltpu.CompilerParams(
            dimension_semantics=("arbitrary", "arbitrary", "arbitrary"), vmem_limit_bytes=VMEM_LIMIT),
        name="attn_b",
    )(_tile_scalars(qn), _tile_scalars(kn), qbt, kb, vbt)


def _merge_kernel(x_ref, oa_ref, ob_ref, ga_ref, gb_ref, wa_ref, wb_ref, wo_ref, h_ref):
    ya = jnp.dot(oa_ref[0], wa_ref[...], preferred_element_type=F32)
    yb = jnp.dot(ob_ref[0], wb_ref[...], preferred_element_type=F32)
    merged = ga_ref[0] * ya + gb_ref[0] * yb
    h_ref[0] = x_ref[0] + jnp.dot(merged.astype(BF16), wo_ref[...], preferred_element_type=F32)


def _merge(x, oa, ob, ga, gb, wa, wb, wo):
    b, s, d = x.shape
    t = min(MERGE_TILE, s)
    tile = pl.BlockSpec((1, t, d), lambda bi, i: (bi, i, 0))
    return pl.pallas_call(
        _merge_kernel,
        grid=(b, s // t),
        in_specs=[tile, tile, tile, tile, tile,
                  _const_spec(wa.shape), _const_spec(wb.shape), _const_spec(wo.shape)],
        out_specs=tile,
        out_shape=jax.ShapeDtypeStruct((b, s, d), F32),
        compiler_params=pltpu.CompilerParams(
            dimension_semantics=("arbitrary", "arbitrary"), vmem_limit_bytes=VMEM_LIMIT),
        name="merge",
    )(x, oa, ob, ga, gb, wa, wb, wo)


def _ffn_kernel(hp_ref, h_ref, hn_ref, p_ref, gffn_ref, wup_ref, cw_ref, cb_ref, wdn_ref,
                wple_ref, gple_ref, wpg_ref, gfin_ref, y_ref, ug_ref, uv_ref, a_ref):
    h = h_ref[0]
    t = h.shape[0]
    rows = t + 2 * HALO
    first_row = pl.program_id(1) * t
    seq_len = pl.num_programs(1) * t
    hh = jnp.concatenate([hp_ref[0], h, hn_ref[0]], axis=0)
    r = lax.broadcasted_iota(jnp.int32, (rows, 1), 0) + (first_row - HALO)
    inside = jnp.logical_and(r >= 0, r < seq_len)
    n2 = jnp.where(inside, _rms(hh, gffn_ref[...]), 0.0).astype(BF16)

    def up_proj(c):
        c0 = c * FFN_CHUNK
        ug_ref[c % 2] = jnp.dot(n2, wup_ref[:, c0:c0 + FFN_CHUNK], preferred_element_type=F32)
        uv_ref[c % 2] = jnp.dot(n2, wup_ref[:, D_FF + c0:D_FF + c0 + FFN_CHUNK],
                                preferred_element_type=F32)

    def conv(u_ref, slot, c0):
        w = cw_ref[:, c0:c0 + FFN_CHUNK]
        um = u_ref[slot, HALO - 1:HALO - 1 + t, :]
        uc = u_ref[slot, HALO:HALO + t, :]
        up = u_ref[slot, HALO + 1:HALO + 1 + t, :]
        return um * w[0:1] + uc * w[1:2] + up * w[2:3] + cb_ref[:, c0:c0 + FFN_CHUNK]

    n_chunks = D_FF // FFN_CHUNK
    split = (n_chunks + 1) // 2 * FFN_CHUNK
    up_proj(0)
    for c in range(n_chunks):
        if c + 1 < n_chunks:
            up_proj(c + 1)
        c0 = c * FFN_CHUNK
        a = jax.nn.gelu(conv(ug_ref, c % 2, c0)) * conv(uv_ref, c % 2, D_FF + c0)
        a_ref[:, c0:c0 + FFN_CHUNK] = a.astype(BF16)
        if c0 + FFN_CHUNK == split:
            down = jnp.dot(a_ref[:, :split], wdn_ref[:split, :], preferred_element_type=F32)
    down = down + jnp.dot(a_ref[:, split:], wdn_ref[split:, :], preferred_element_type=F32)
    h2 = h + down

    n3 = _rms(h2, gple_ref[...]).astype(BF16)
    gate = jax.nn.sigmoid(jnp.dot(n3, wpg_ref[...], preferred_element_type=F32))
    e = jnp.dot(p_ref[0].astype(BF16), wple_ref[...], preferred_element_type=F32)
    h3 = h2 + e * gate
    y_ref[0] = _rms(h3, gfin_ref[...])


def _ffn(h, p, g_ffn, wup, conv_w, conv_b, wdn, wple, g_ple, wpg, g_final):
    b, s, d = h.shape
    t = min(FFN_TILE, s)
    per = t // HALO
    nhb = s // HALO
    tile = pl.BlockSpec((1, t, d), lambda bi, i: (bi, i, 0))
    return pl.pallas_call(
        _ffn_kernel,
        grid=(b, s // t),
        in_specs=[
            pl.BlockSpec((1, HALO, d), lambda bi, i: (bi, jnp.maximum(i * per - 1, 0), 0)),
            tile,
            pl.BlockSpec((1, HALO, d), lambda bi, i: (bi, jnp.minimum((i + 1) * per, nhb - 1), 0)),
            pl.BlockSpec((1, t, PLE_DIM), lambda bi, i: (bi, i, 0)),
            _const_spec((1, d)),
            _const_spec(wup.shape),
            _const_spec(conv_w.shape),
            _const_spec(conv_b.shape),
            _const_spec(wdn.shape),
            _const_spec(wple.shape),
            _const_spec((1, d)),
            _const_spec(wpg.shape),
            _const_spec((1, d)),
        ],
        out_specs=tile,
        out_shape=jax.ShapeDtypeStruct((b, s, d), F32),
        scratch_shapes=[pltpu.VMEM((2, t + 2 * HALO, FFN_CHUNK), F32),
                        pltpu.VMEM((2, t + 2 * HALO, FFN_CHUNK), F32),
                        pltpu.VMEM((t, D_FF), BF16)],
        compiler_params=pltpu.CompilerParams(
            dimension_semantics=("arbitrary", "arbitrary"), vmem_limit_bytes=VMEM_LIMIT),
        name="ffn",
    )(h, h, h, p, g_ffn, wup, conv_w, conv_b, wdn, wple, g_ple, wpg, g_final)


def _rope_tables_t(s):
    pos = jnp.arange(s, dtype=jnp.int32)

    def tab(ids, dim, theta):
        inv = theta ** (-jnp.arange(0, dim, 2, dtype=F32) / dim)
        ang = ids.astype(F32)[:, None] * inv[None, :]
        return jnp.cos(ang).T, jnp.sin(ang).T

    ca, sa = tab(pos, A_ROT_DIM, A_ROPE_THETA)
    cr, sr = tab(pos // GRID_W, B_HEAD_DIM // 2, B_AXIAL_THETA)
    cc, sc = tab(pos % GRID_W, B_HEAD_DIM // 2, B_AXIAL_THETA)
    return jnp.concatenate([ca, cr, cc], axis=0), jnp.concatenate([sa, sr, sc], axis=0)


def _encode(x, p, cos_t, sin_t, w):
    qat, ka, vat, qbt, kb, vbt, ga, gb, qna, kna, qnb, knb = _project(
        x, cos_t, sin_t, w["g_mix"], w["wt"], w["wg"], w["g_qn"], w["g_kn"])
    oa = _attention_a(w["lq1"], w["lk1"], w["lq2"], w["lk2"], w["g_diff"], qat, ka, vat, qna, kna)
    ob = _attention_b(qbt, kb, vbt, qnb, knb)
    h1 = _merge(x, oa, ob, ga, gb, w["w_a"], w["w_b"], w["w_out"])
    return _ffn(h1, p, w["g_ffn"], w["w_up"], w["conv_w"], w["conv_b"], w["w_down"],
                w["w_ple"], w["g_ple"], w["w_ple_gate"], w["g_final"])


def kernel(x_prompt, x_sample, p_prompt, p_sample, g_mix, w_in, lambda_q1, lambda_k1, lambda_q2,
           lambda_k2, g_diff, w_a, g_qn, g_kn, w_b, w_out, g_ffn, w_up, conv_w, conv_b, w_down,
           w_ple, g_ple, w_ple_gate, g_final):
    w_in0 = w_in[0]
    row = lambda v: v.reshape(1, -1).astype(F32)
    col = lambda v: v.reshape(-1, 1).astype(F32)
    w = dict(
        g_mix=row(g_mix[0]),
        wt=w_in0[:, :N_TRANSPOSED].T.astype(BF16),
        wg=w_in0[:, N_TRANSPOSED:].astype(BF16),
        g_qn=col(g_qn[0]), g_kn=col(g_kn[0]),
        lq1=row(lambda_q1[0]), lk1=row(lambda_k1[0]), lq2=row(lambda_q2[0]), lk2=row(lambda_k2[0]),
        g_diff=col(g_diff[0]),
        w_a=w_a[0].astype(BF16), w_b=w_b[0].astype(BF16), w_out=w_out[0].astype(BF16),
        g_ffn=row(g_ffn[0]),
        w_up=w_up[0].astype(BF16), conv_w=conv_w[0].astype(F32), conv_b=row(conv_b[0]),
        w_down=w_down[0].astype(BF16),
        w_ple=w_ple[0].astype(BF16), g_ple=row(g_ple[0]), w_ple_gate=w_ple_gate[0].astype(BF16),
        g_final=row(g_final),
    )
    outs = []
    for x, p in ((x_prompt, p_prompt), (x_sample, p_sample)):
        cos_t, sin_t = _rope_tables_t(x.shape[1])
        outs.append(_encode(x, p[0], cos_t, sin_t, w))
    return tuple(outs)
```

```python
import functools
import math

import jax
import jax.numpy as jnp
from jax import lax
from jax.experimental import pallas as pl
from jax.experimental.pallas import tpu as pltpu

F32 = jnp.float32
BF16 = jnp.bfloat16

D_MODEL = 1024
PLE_DIM = 256
GRID_W = 64
EPS = 1e-6
A_HEADS = 8
A_HEAD_DIM = 64
A_ROT_DIM = A_HEAD_DIM // 4
A_ROPE_THETA = 500000.0
B_HEADS = 16
B_KV_HEADS = 4
B_HEAD_DIM = 64
B_REP = B_HEADS // B_KV_HEADS
B_AXIAL_THETA = 10000.0
D_FF = 2816
LAM_INIT = 0.8 - 0.6 * math.exp(-0.3 * 0)

A_W = A_HEADS * 2 * A_HEAD_DIM
B_Q = B_HEADS * B_HEAD_DIM
B_KV = B_KV_HEADS * B_HEAD_DIM
N_TRANSPOSED = 3 * A_W + B_Q + 2 * B_KV

V7X_VMEM_BYTES = 64 * 1024 * 1024
VMEM_LIMIT = V7X_VMEM_BYTES * 7 // 8
SUBLANES = 8
LANES = 128

TOKEN_TILE = 512
A_Q_TILE = 512
B_Q_TILE = 256
MERGE_TILE = 1024
FFN_TILE = 512
FFN_CHUNK = 256
HALO = SUBLANES
SUM_ROWS = 16
A_VROWS = 2 * A_HEAD_DIM + SUM_ROWS
B_VROWS = 112
NEG_BIG = -1e30


def _rms(x, g):
    ms = jnp.mean(x * x, axis=-1, keepdims=True)
    return (x * lax.rsqrt(ms + EPS)) * g


def _rms_rows(x, g):
    ms = jnp.mean(x * x, axis=0, keepdims=True)
    return (x * lax.rsqrt(ms + EPS)) * g


def _proj_kernel(x_ref, cos_ref, sin_ref, gmix_ref, wt_ref, wg_ref, gq_ref, gk_ref,
                 qat_ref, ka_ref, vat_ref, qbt_ref, kb_ref, vbt_ref, ga_ref, gb_ref,
                 qna_ref, kna_ref, qnb_ref, knb_ref):
    x = x_ref[0]
    n = _rms(x, gmix_ref[...])
    nb = n.astype(BF16)
    nt = n.T.astype(BF16)

    cos = cos_ref[...]
    sin = sin_ref[...]
    ca, sa = cos[0:8], sin[0:8]
    cr, sr = cos[8:24], sin[8:24]
    cc, sc = cos[24:40], sin[24:40]

    def rope_a(p):
        x1, x2 = p[0:8], p[8:16]
        return jnp.concatenate([x1 * ca - x2 * sa, x2 * ca + x1 * sa, p[16:64]], axis=0)

    def rope_b(p):
        a1, a2, b1, b2 = p[0:16], p[16:32], p[32:48], p[48:64]
        return jnp.concatenate([a1 * cr - a2 * sr, a2 * cr + a1 * sr,
                                b1 * cc - b2 * sc, b2 * cc + b1 * sc], axis=0)

    def proj_t(lo, hi):
        return jnp.dot(wt_ref[lo:hi, :], nt, preferred_element_type=F32)

    def max_sq_norm(*vecs_t):
        sq = [jnp.max(jnp.sum(v * v, axis=0, keepdims=True)) for v in vecs_t]
        return jnp.full((SUBLANES, LANES), functools.reduce(jnp.maximum, sq), F32)

    scale = A_HEAD_DIM ** -0.5 * math.log2(math.e)


    def put_aq(p):
        for h in range(A_HEADS):
            q1 = rope_a(p[h * 128:h * 128 + 64]) * scale
            q2 = rope_a(p[h * 128 + 64:(h + 1) * 128]) * scale
            qat_ref[0, h * 128:(h + 1) * 128, :] = jnp.concatenate([q1, q2], axis=0).astype(BF16)
            qna_ref[0, h] = max_sq_norm(q1, q2)

    def put_ak(p):
        for s in range(A_W // LANES):
            blk = jnp.concatenate([rope_a(p[s * 128:s * 128 + 64]),
                                   rope_a(p[s * 128 + 64:(s + 1) * 128])], axis=0)
            ka_ref[0, :, s * LANES:(s + 1) * LANES] = blk.T.astype(BF16)
            kna_ref[0, s] = max_sq_norm(blk)

    def put_v(p, v_ref, heads, dv, vrows):
        ones_rows = (lax.broadcasted_iota(jnp.int32, (vrows - dv, nt.shape[1]), 0) == 0).astype(BF16)
        for h in range(heads):
            v_ref[0, 0, h * vrows:h * vrows + dv, :] = p[h * dv:(h + 1) * dv].astype(BF16)
            v_ref[0, 0, h * vrows + dv:(h + 1) * vrows, :] = ones_rows

    def put_bq(p):
        gq = gq_ref[...]
        for g in range(B_KV_HEADS):
            ys = [rope_b(_rms_rows(p[h * 64:(h + 1) * 64], gq)) * scale
                  for h in range(g * B_REP, (g + 1) * B_REP)]
            qbt_ref[0, g * B_REP * 64:(g + 1) * B_REP * 64, :] = jnp.concatenate(ys, axis=0).astype(BF16)
            qnb_ref[0, g] = max_sq_norm(*ys)

    def put_bk(p):
        gk = gk_ref[...]
        zeros = jnp.zeros((LANES - B_HEAD_DIM, p.shape[1]), F32)
        for g in range(B_KV_HEADS):
            y = rope_b(_rms_rows(p[g * 64:(g + 1) * 64], gk))
            kb_ref[0, g] = jnp.concatenate([y, zeros], axis=0).T.astype(BF16)
            knb_ref[0, g] = max_sq_norm(y)

    off_bq = 3 * A_W
    off_bk = off_bq + B_Q
    p_aq = proj_t(0, A_W)
    p_ak = proj_t(A_W, 2 * A_W)
    put_aq(p_aq)
    p_av = proj_t(2 * A_W, off_bq)
    put_ak(p_ak)
    p_bq = proj_t(off_bq, off_bk)
    put_v(p_av, vat_ref, A_HEADS, 2 * A_HEAD_DIM, A_VROWS)
    p_bkv = proj_t(off_bk, off_bk + 2 * B_KV)
    put_bq(p_bq)
    gates = jnp.dot(nb, wg_ref[...], preferred_element_type=F32)
    put_bk(p_bkv[:B_KV])
    put_v(p_bkv[B_KV:], vbt_ref, B_KV_HEADS, B_HEAD_DIM, B_VROWS)
    ga_ref[0] = jax.nn.sigmoid(gates[:, :D_MODEL])
    gb_ref[0] = jax.nn.sigmoid(gates[:, D_MODEL:])


def _const_spec(shape):
    nd = len(shape)
    return pl.BlockSpec(shape, lambda *_: (0,) * nd, pipeline_mode=pl.Buffered(1))


def _project(x, cos_t, sin_t, g_mix, wt, wg, gq, gk):
    b, s, d = x.shape
    t = min(TOKEN_TILE, s)
    nt = s // t
    out_shape = (
        jax.ShapeDtypeStruct((b, A_W, s), BF16),
        jax.ShapeDtypeStruct((b, s, A_W), BF16),
        jax.ShapeDtypeStruct((b, nt, A_HEADS * A_VROWS, t), BF16),
        jax.ShapeDtypeStruct((b, B_Q, s), BF16),
        jax.ShapeDtypeStruct((b, B_KV_HEADS, s, LANES), BF16),
        jax.ShapeDtypeStruct((b, nt, B_KV_HEADS * B_VROWS, t), BF16),
        jax.ShapeDtypeStruct((b, s, D_MODEL), F32),
        jax.ShapeDtypeStruct((b, s, D_MODEL), F32),
        jax.ShapeDtypeStruct((b, A_HEADS, nt * SUBLANES, LANES), F32),
        jax.ShapeDtypeStruct((b, A_HEADS, nt * SUBLANES, LANES), F32),
        jax.ShapeDtypeStruct((b, B_KV_HEADS, nt * SUBLANES, LANES), F32),
        jax.ShapeDtypeStruct((b, B_KV_HEADS, nt * SUBLANES, LANES), F32),
    )
    in_specs = [
        pl.BlockSpec((1, t, d), lambda bi, i: (bi, i, 0)),
        pl.BlockSpec((40, t), lambda bi, i: (0, i)),
        pl.BlockSpec((40, t), lambda bi, i: (0, i)),
        _const_spec((1, d)),
        _const_spec(wt.shape),
        _const_spec(wg.shape),
        _const_spec((B_HEAD_DIM, 1)),
        _const_spec((B_HEAD_DIM, 1)),
    ]
    out_specs = (
        pl.BlockSpec((1, A_W, t), lambda bi, i: (bi, 0, i)),
        pl.BlockSpec((1, t, A_W), lambda bi, i: (bi, i, 0)),
        pl.BlockSpec((1, 1, A_HEADS * A_VROWS, t), lambda bi, i: (bi, i, 0, 0)),
        pl.BlockSpec((1, B_Q, t), lambda bi, i: (bi, 0, i)),
        pl.BlockSpec((1, B_KV_HEADS, t, LANES), lambda bi, i: (bi, 0, i, 0)),
        pl.BlockSpec((1, 1, B_KV_HEADS * B_VROWS, t), lambda bi, i: (bi, i, 0, 0)),
        pl.BlockSpec((1, t, D_MODEL), lambda bi, i: (bi, i, 0)),
        pl.BlockSpec((1, t, D_MODEL), lambda bi, i: (bi, i, 0)),
        pl.BlockSpec((1, A_HEADS, SUBLANES, LANES), lambda bi, i: (bi, 0, i, 0)),
        pl.BlockSpec((1, A_HEADS, SUBLANES, LANES), lambda bi, i: (bi, 0, i, 0)),
        pl.BlockSpec((1, B_KV_HEADS, SUBLANES, LANES), lambda bi, i: (bi, 0, i, 0)),
        pl.BlockSpec((1, B_KV_HEADS, SUBLANES, LANES), lambda bi, i: (bi, 0, i, 0)),
    )
    return pl.pallas_call(
        _proj_kernel,
        grid=(b, nt),
        in_specs=in_specs,
        out_specs=out_specs,
        out_shape=out_shape,
        compiler_params=pltpu.CompilerParams(
            dimension_semantics=("arbitrary", "arbitrary"), vmem_limit_bytes=VMEM_LIMIT),
        name="proj",
    )(x, cos_t, sin_t, g_mix, wt, wg, gq, gk)


KEY_BLOCK = 256


def _flash_t(q_t, k_rows, v_tile_t, s_refs, acc_ref, n_tiles, track_max):
    s_a, s_b = s_refs
    tk = s_a.shape[0]
    acc_ref[...] = jnp.zeros_like(acc_ref)

    def scores(j, s_ref):
        s = jnp.dot(k_rows(pl.multiple_of(j * tk, tk), tk), q_t, preferred_element_type=F32)
        s_ref[...] = s
        return jnp.max(s, axis=0, keepdims=True) if track_max else None

    def consume(v_t, s_ref, m, bmax):
        if not track_max:
            p = jnp.exp2(s_ref[...]).astype(BF16)
            acc_ref[...] += jnp.dot(v_t, p, preferred_element_type=F32)
            return m
        m_new = jnp.maximum(m, bmax)
        alpha = jnp.exp2(m - m_new)
        p = jnp.exp2(s_ref[...] - m_new).astype(BF16)
        acc_ref[...] = alpha * acc_ref[...] + jnp.dot(v_t, p, preferred_element_type=F32)
        return m_new

    def tile(i, carry, last):
        m, bmax_a = carry
        v = v_tile_t(i)
        assert v.shape[1] == 2 * tk
        bmax_b = scores(2 * i + 1, s_b)
        m = consume(v[:, :tk], s_a, m, bmax_a)
        if not last:
            bmax_a = scores(2 * i + 2, s_a)
        m = consume(v[:, tk:], s_b, m, bmax_b)
        return m, bmax_a

    bmax = scores(0, s_a)
    carry = (jnp.full((1, q_t.shape[1]), NEG_BIG, F32), bmax) if track_max else (None, None)
    n_loop = n_tiles - 1
    unroll = next(u for u in ((5, 3, 1) if track_max else (15, 5, 3, 1)) if n_loop % u == 0)
    carry = lax.fori_loop(0, n_loop, lambda i, c: tile(i, c, False), carry, unroll=unroll)
    tile(n_tiles - 1, carry, True)


SCORE_BOUND = 64.0
BOUND_SLACK = 1.05


def _both_paths(qn_ref, kn_ref, q_tiles, n_tiles, body):
    b, h = pl.program_id(0), pl.program_id(1)
    q_sq = functools.reduce(jnp.maximum, [qn_ref[b, h, t] for t in q_tiles])
    k_sq = functools.reduce(jnp.maximum, [kn_ref[b, h, t] for t in range(n_tiles)])
    bounded = q_sq * k_sq * (BOUND_SLACK * BOUND_SLACK) <= SCORE_BOUND * SCORE_BOUND
    pl.when(bounded)(functools.partial(body, False))
    pl.when(jnp.logical_not(bounded))(functools.partial(body, True))


Q_SUBTILES = 2


def _attn_a_kernel(qn_ref, kn_ref, lq1_ref, lk1_ref, lq2_ref, lk2_ref, gd_ref, qt_ref, k_ref, vt_ref,
                   o_ref, s_a, s_b, *acc_refs):
    tq = qt_ref.shape[2] // Q_SUBTILES
    n_tiles, tile = vt_ref.shape[1], vt_ref.shape[3]
    first = pl.program_id(2) * Q_SUBTILES

    def body(track_max):
        for sub, acc_ref in enumerate(acc_refs):
            q = qt_ref[0, :, sub * tq:(sub + 1) * tq]
            row = lax.broadcasted_iota(jnp.int32, q.shape, 0)
            zero = jnp.zeros_like(q)
            q_t = jnp.concatenate([jnp.where(row < A_HEAD_DIM, q, zero),
                                   jnp.where(row >= A_HEAD_DIM, q, zero)], axis=1)
            _flash_t(q_t, lambda r0, n: k_ref[0, pl.ds(r0, n), :], lambda i: vt_ref[0, i],
                     (s_a, s_b), acc_ref, n_tiles, track_max)

            lam = (jnp.exp(jnp.sum(lq1_ref[...] * lk1_ref[...], axis=-1, keepdims=True))
                   - jnp.exp(jnp.sum(lq2_ref[...] * lk2_ref[...], axis=-1, keepdims=True))
                   + LAM_INIT)
            dv = 2 * A_HEAD_DIM
            acc = acc_ref[0:dv, :]
            l = acc_ref[dv:dv + 1, :]
            o = acc[:, :tq] / l[:, :tq] - lam * (acc[:, tq:] / l[:, tq:])
            y = _rms_rows(o, gd_ref[...]) * (1.0 - LAM_INIT)
            o_ref[0, sub * tq:(sub + 1) * tq, :] = y.T.astype(BF16)

    _both_paths(qn_ref, kn_ref, [(first + sub) * tq // tile for sub in range(Q_SUBTILES)], n_tiles, body)


def _attn_b_kernel(qn_ref, kn_ref, qt_ref, k_ref, vt_ref, o_ref, s_a, s_b, *acc_refs):
    tq = qt_ref.shape[2] // Q_SUBTILES
    n_tiles, tile = vt_ref.shape[1], vt_ref.shape[3]
    first = pl.program_id(2) * Q_SUBTILES

    def body(track_max):
        for sub, acc_ref in enumerate(acc_refs):
            q = qt_ref[0, :, sub * tq:(sub + 1) * tq]
            q_cat = jnp.concatenate([q[r * 64:(r + 1) * 64] for r in range(B_REP)], axis=1)
            q_t = jnp.concatenate([q_cat, jnp.zeros_like(q_cat)], axis=0)
            _flash_t(q_t, lambda r0, n: k_ref[0, 0, pl.ds(r0, n), :], lambda i: vt_ref[0, i],
                     (s_a, s_b), acc_ref, n_tiles, track_max)

            o = acc_ref[0:B_HEAD_DIM, :] / acc_ref[B_HEAD_DIM:B_HEAD_DIM + 1, :]
            o_rows = jnp.concatenate([o[:, r * tq:(r + 1) * tq] for r in range(B_REP)], axis=0)
            o_ref[0, sub * tq:(sub + 1) * tq, :] = o_rows.T.astype(BF16)

    _both_paths(qn_ref, kn_ref, [(first + sub) * tq // tile for sub in range(Q_SUBTILES)], n_tiles, body)


def _tile_scalars(stat):
    return stat[:, :, ::SUBLANES, 0]


def _attention_a(lq1, lk1, lq2, lk2, g_diff, qat, ka, vat, qn, kn):
    b, _, s = qat.shape
    tk = vat.shape[3]
    tq = A_Q_TILE
    tstep = Q_SUBTILES * tq
    small = lambda shape: pl.BlockSpec(shape, lambda bi, h, i, *_: (0,) * len(shape))
    return pl.pallas_call(
        _attn_a_kernel,
        grid_spec=pltpu.PrefetchScalarGridSpec(
            num_scalar_prefetch=2,
            grid=(b, A_HEADS, s // tstep),
            in_specs=[
                small((1, A_HEAD_DIM)), small((1, A_HEAD_DIM)), small((1, A_HEAD_DIM)), small((1, A_HEAD_DIM)),
                small((2 * A_HEAD_DIM, 1)),
                pl.BlockSpec((1, 2 * A_HEAD_DIM, tstep), lambda bi, h, i, *_: (bi, h, i)),
                pl.BlockSpec((1, s, 2 * A_HEAD_DIM), lambda bi, h, i, *_: (bi, 0, h)),
                pl.BlockSpec((1, s // tk, A_VROWS, tk), lambda bi, h, i, *_: (bi, 0, h, 0)),
            ],
            out_specs=pl.BlockSpec((1, tstep, 2 * A_HEAD_DIM), lambda bi, h, i, *_: (bi, i, h)),
            scratch_shapes=([pltpu.VMEM((KEY_BLOCK, 2 * tq), F32)] * 2
                            + [pltpu.VMEM((A_VROWS, 2 * tq), F32)] * Q_SUBTILES)),
        out_shape=jax.ShapeDtypeStruct((b, s, A_W), BF16),
        compiler_params=pltpu.CompilerParams(
            dimension_semantics=("arbitrary", "arbitrary", "arbitrary"), vmem_limit_bytes=VMEM_LIMIT),
        name="attn_a",
    )(_tile_scalars(qn), _tile_scalars(kn), lq1, lk1, lq2, lk2, g_diff, qat, ka, vat)


def _attention_b(qbt, kb, vbt, qn, kn):
    b, _, s = qbt.shape
    tk = vbt.shape[3]
    tq = B_Q_TILE
    tstep = Q_SUBTILES * tq
    rows = B_REP * B_HEAD_DIM
    return pl.pallas_call(
        _attn_b_kernel,
        grid_spec=pltpu.PrefetchScalarGridSpec(
            num_scalar_prefetch=2,
            grid=(b, B_KV_HEADS, s // tstep),
            in_specs=[
                pl.BlockSpec((1, rows, tstep), lambda bi, g, i, *_: (bi, g, i)),
                pl.BlockSpec((1, 1, s, LANES), lambda bi, g, i, *_: (bi, g, 0, 0)),
                pl.BlockSpec((1, s // tk, B_VROWS, tk), lambda bi, g, i, *_: (bi, 0, g, 0)),
            ],
            out_specs=pl.BlockSpec((1, tstep, rows), lambda bi, g, i, *_: (bi, i, g)),
            scratch_shapes=([pltpu.VMEM((KEY_BLOCK, B_REP * tq), F32)] * 2
                            + [pltpu.VMEM((B_VROWS, B_REP * tq), F32)] * Q_SUBTILES)),
        out_shape=jax.ShapeDtypeStruct((b, s, B_Q), BF16),
        compiler_params=pltpu.CompilerParams(
            dimension_semantics=("arbitrary", "arbitrary", "arbitrary"), vmem_limit_bytes=VMEM_LIMIT),
        name="attn_b",
    )(_tile_scalars(qn), _tile_scalars(kn), qbt, kb, vbt)


def _merge_kernel(x_ref, oa_ref, ob_ref, ga_ref, gb_ref, wa_ref, wb_ref, wo_ref, h_ref):
    ya = jnp.dot(oa_ref[0], wa_ref[...], preferred_element_type=F32)
    yb = jnp.dot(ob_ref[0], wb_ref[...], preferred_element_type=F32)
    merged = ga_ref[0] * ya + gb_ref[0] * yb
    h_ref[0] = x_ref[0] + jnp.dot(merged.astype(BF16), wo_ref[...], preferred_element_type=F32)


def _merge(x, oa, ob, ga, gb, wa, wb, wo):
    b, s, d = x.shape
    t = min(MERGE_TILE, s)
    tile = pl.BlockSpec((1, t, d), lambda bi, i: (bi, i, 0))
    return pl.pallas_call(
        _merge_kernel,
        grid=(b, s // t),
        in_specs=[tile, tile, tile, tile, tile,
                  _const_spec(wa.shape), _const_spec(wb.shape), _const_spec(wo.shape)],
        out_specs=tile,
        out_shape=jax.ShapeDtypeStruct((b, s, d), F32),
        compiler_params=pltpu.CompilerParams(
            dimension_semantics=("arbitrary", "arbitrary"), vmem_limit_bytes=VMEM_LIMIT),
        name="merge",
    )(x, oa, ob, ga, gb, wa, wb, wo)


def _ffn_kernel(hp_ref, h_ref, hn_ref, p_ref, gffn_ref, wup_ref, cw_ref, cb_ref, wdn_ref,
                wple_ref, gple_ref, wpg_ref, gfin_ref, y_ref, ug_ref, uv_ref, a_ref):
    h = h_ref[0]
    t = h.shape[0]
    rows = t + 2 * HALO
    first_row = pl.program_id(1) * t
    seq_len = pl.num_programs(1) * t
    hh = jnp.concatenate([hp_ref[0], h, hn_ref[0]], axis=0)
    r = lax.broadcasted_iota(jnp.int32, (rows, 1), 0) + (first_row - HALO)
    inside = jnp.logical_and(r >= 0, r < seq_len)
    n2 = jnp.where(inside, _rms(hh, gffn_ref[...]), 0.0).astype(BF16)

    def up_proj(c):
        c0 = c * FFN_CHUNK
        ug_ref[c % 2] = jnp.dot(n2, wup_ref[:, c0:c0 + FFN_CHUNK], preferred_element_type=F32)
        uv_ref[c % 2] = jnp.dot(n2, wup_ref[:, D_FF + c0:D_FF + c0 + FFN_CHUNK],
                                preferred_element_type=F32)

    def conv(u_ref, slot, c0):
        w = cw_ref[:, c0:c0 + FFN_CHUNK]
        um = u_ref[slot, HALO - 1:HALO - 1 + t, :]
        uc = u_ref[slot, HALO:HALO + t, :]
        up = u_ref[slot, HALO + 1:HALO + 1 + t, :]
        return um * w[0:1] + uc * w[1:2] + up * w[2:3] + cb_ref[:, c0:c0 + FFN_CHUNK]

    n_chunks = D_FF // FFN_CHUNK
    split = (n_chunks + 1) // 2 * FFN_CHUNK
    up_proj(0)
    for c in range(n_chunks):
        if c + 1 < n_chunks:
            up_proj(c + 1)
        c0 = c * FFN_CHUNK
        a = jax.nn.gelu(conv(ug_ref, c % 2, c0)) * conv(uv_ref, c % 2, D_FF + c0)
        a_ref[:, c0:c0 + FFN_CHUNK] = a.astype(BF16)
        if c0 + FFN_CHUNK == split:
            down = jnp.dot(a_ref[:, :split], wdn_ref[:split, :], preferred_element_type=F32)
    down = down + jnp.dot(a_ref[:, split:], wdn_ref[split:, :], preferred_element_type=F32)
    h2 = h + down

    n3 = _rms(h2, gple_ref[...]).astype(BF16)
    gate = jax.nn.sigmoid(jnp.dot(n3, wpg_ref[...], preferred_element_type=F32))
    e = jnp.dot(p_ref[0].astype(BF16), wple_ref[...], preferred_element_type=F32)
    h3 = h2 + e * gate
    y_ref[0] = _rms(h3, gfin_ref[...])


def _ffn(h, p, g_ffn, wup, conv_w, conv_b, wdn, wple, g_ple, wpg, g_final):
    b, s, d = h.shape
    t = min(FFN_TILE, s)
    per = t // HALO
    nhb = s // HALO
    tile = pl.BlockSpec((1, t, d), lambda bi, i: (bi, i, 0))
    return pl.pallas_call(
        _ffn_kernel,
        grid=(b, s // t),
        in_specs=[
            pl.BlockSpec((1, HALO, d), lambda bi, i: (bi, jnp.maximum(i * per - 1, 0), 0)),
            tile,
            pl.BlockSpec((1, HALO, d), lambda bi, i: (bi, jnp.minimum((i + 1) * per, nhb - 1), 0)),
            pl.BlockSpec((1, t, PLE_DIM), lambda bi, i: (bi, i, 0)),
            _const_spec((1, d)),
            _const_spec(wup.shape),
            _const_spec(conv_w.shape),
            _const_spec(conv_b.shape),
            _const_spec(wdn.shape),
            _const_spec(wple.shape),
            _const_spec((1, d)),
            _const_spec(wpg.shape),
            _const_spec((1, d)),
        ],
        out_specs=tile,
        out_shape=jax.ShapeDtypeStruct((b, s, d), F32),
        scratch_shapes=[pltpu.VMEM((2, t + 2 * HALO, FFN_CHUNK), F32),
                        pltpu.VMEM((2, t + 2 * HALO, FFN_CHUNK), F32),
                        pltpu.VMEM((t, D_FF), BF16)],
        compiler_params=pltpu.CompilerParams(
            dimension_semantics=("arbitrary", "arbitrary"), vmem_limit_bytes=VMEM_LIMIT),
        name="ffn",
    )(h, h, h, p, g_ffn, wup, conv_w, conv_b, wdn, wple, g_ple, wpg, g_final)


def _rope_tables_t(s):
    pos = jnp.arange(s, dtype=jnp.int32)

    def tab(ids, dim, theta):
        inv = theta ** (-jnp.arange(0, dim, 2, dtype=F32) / dim)
        ang = ids.astype(F32)[:, None] * inv[None, :]
        return jnp.cos(ang).T, jnp.sin(ang).T

    ca, sa = tab(pos, A_ROT_DIM, A_ROPE_THETA)
    cr, sr = tab(pos // GRID_W, B_HEAD_DIM // 2, B_AXIAL_THETA)
    cc, sc = tab(pos % GRID_W, B_HEAD_DIM // 2, B_AXIAL_THETA)
    return jnp.concatenate([ca, cr, cc], axis=0), jnp.concatenate([sa, sr, sc], axis=0)


def _encode(x, p, cos_t, sin_t, w):
    qat, ka, vat, qbt, kb, vbt, ga, gb, qna, kna, qnb, knb = _project(
        x, cos_t, sin_t, w["g_mix"], w["wt"], w["wg"], w["g_qn"], w["g_kn"])
    oa = _attention_a(w["lq1"], w["lk1"], w["lq2"], w["lk2"], w["g_diff"], qat, ka, vat, qna, kna)
    ob = _attention_b(qbt, kb, vbt, qnb, knb)
    h1 = _merge(x, oa, ob, ga, gb, w["w_a"], w["w_b"], w["w_out"])
    return _ffn(h1, p, w["g_ffn"], w["w_up"], w["conv_w"], w["conv_b"], w["w_down"],
                w["w_ple"], w["g_ple"], w["w_ple_gate"], w["g_final"])


def kernel(x_prompt, x_sample, p_prompt, p_sample, g_mix, w_in, lambda_q1, lambda_k1, lambda_q2,
           lambda_k2, g_diff, w_a, g_qn, g_kn, w_b, w_out, g_ffn, w_up, conv_w, conv_b, w_down,
           w_ple, g_ple, w_ple_gate, g_final):
    w_in0 = w_in[0]
    row = lambda v: v.reshape(1, -1).astype(F32)
    col = lambda v: v.reshape(-1, 1).astype(F32)
    w = dict(
        g_mix=row(g_mix[0]),
        wt=w_in0[:, :N_TRANSPOSED].T.astype(BF16),
        wg=w_in0[:, N_TRANSPOSED:].astype(BF16),
        g_qn=col(g_qn[0]), g_kn=col(g_kn[0]),
        lq1=row(lambda_q1[0]), lk1=row(lambda_k1[0]), lq2=row(lambda_q2[0]), lk2=row(lambda_k2[0]),
        g_diff=col(g_diff[0]),
        w_a=w_a[0].astype(BF16), w_b=w_b[0].astype(BF16), w_out=w_out[0].astype(BF16),
        g_ffn=row(g_ffn[0]),
        w_up=w_up[0].astype(BF16), conv_w=conv_w[0].astype(F32), conv_b=row(conv_b[0]),
        w_down=w_down[0].astype(BF16),
        w_ple=w_ple[0].astype(BF16), g_ple=row(g_ple[0]), w_ple_gate=w_ple_gate[0].astype(BF16),
        g_final=row(g_final),
    )
    outs = []
    for x, p in ((x_prompt, p_prompt), (x_sample, p_sample)):
        cos_t, sin_t = _rope_tables_t(x.shape[1])
        outs.append(_encode(x, p[0], cos_t, sin_t, w))
    return tuple(outs)
```

```python
import functools
import math

import jax
import jax.numpy as jnp
from jax import lax
from jax.experimental import pallas as pl
from jax.experimental.pallas import tpu as pltpu

F32 = jnp.float32
BF16 = jnp.bfloat16

D_MODEL = 1024
PLE_DIM = 256
GRID_W = 64
EPS = 1e-6
A_HEADS = 8
A_HEAD_DIM = 64
A_ROT_DIM = A_HEAD_DIM // 4
A_ROPE_THETA = 500000.0
B_HEADS = 16
B_KV_HEADS = 4
B_HEAD_DIM = 64
B_REP = B_HEADS // B_KV_HEADS
B_AXIAL_THETA = 10000.0
D_FF = 2816
LAM_INIT = 0.8 - 0.6 * math.exp(-0.3 * 0)

A_W = A_HEADS * 2 * A_HEAD_DIM
B_Q = B_HEADS * B_HEAD_DIM
B_KV = B_KV_HEADS * B_HEAD_DIM
N_TRANSPOSED = 3 * A_W + B_Q + 2 * B_KV

V7X_VMEM_BYTES = 64 * 1024 * 1024
VMEM_LIMIT = V7X_VMEM_BYTES * 7 // 8
SUBLANES = 8
LANES = 128

TOKEN_TILE = 512
A_Q_TILE = 512
B_Q_TILE = 256
MERGE_TILE = 1024
FFN_TILE = 512
FFN_CHUNK = 256
HALO = SUBLANES
SUM_ROWS = 16
A_VROWS = 2 * A_HEAD_DIM + SUM_ROWS
B_VROWS = 112
NEG_BIG = -1e30


def _rms(x, g):
    ms = jnp.mean(x * x, axis=-1, keepdims=True)
    return (x * lax.rsqrt(ms + EPS)) * g


def _rms_rows(x, g):
    ms = jnp.mean(x * x, axis=0, keepdims=True)
    return (x * lax.rsqrt(ms + EPS)) * g


def _proj_kernel(x_ref, cos_ref, sin_ref, gmix_ref, wt_ref, wg_ref, gq_ref, gk_ref,
                 qat_ref, ka_ref, vat_ref, qbt_ref, kb_ref, vbt_ref, ga_ref, gb_ref,
                 qna_ref, kna_ref, qnb_ref, knb_ref):
    x = x_ref[0]
    n = _rms(x, gmix_ref[...])
    nb = n.astype(BF16)
    nt = n.T.astype(BF16)

    cos = cos_ref[...]
    sin = sin_ref[...]
    ca, sa = cos[0:8], sin[0:8]
    cr, sr = cos[8:24], sin[8:24]
    cc, sc = cos[24:40], sin[24:40]

    def rope_a(p):
        x1, x2 = p[0:8], p[8:16]
        return jnp.concatenate([x1 * ca - x2 * sa, x2 * ca + x1 * sa, p[16:64]], axis=0)

    def rope_b(p):
        a1, a2, b1, b2 = p[0:16], p[16:32], p[32:48], p[48:64]
        return jnp.concatenate([a1 * cr - a2 * sr, a2 * cr + a1 * sr,
                                b1 * cc - b2 * sc, b2 * cc + b1 * sc], axis=0)

    def proj_t(lo, hi):
        return jnp.dot(wt_ref[lo:hi, :], nt, preferred_element_type=F32)

    def max_sq_norm(*vecs_t):
        sq = [jnp.max(jnp.sum(v * v, axis=0, keepdims=True)) for v in vecs_t]
        return jnp.full((SUBLANES, LANES), functools.reduce(jnp.maximum, sq), F32)

    scale = A_HEAD_DIM ** -0.5 * math.log2(math.e)


    def put_aq(p):
        for h in range(A_HEADS):
            q1 = rope_a(p[h * 128:h * 128 + 64]) * scale
            q2 = rope_a(p[h * 128 + 64:(h + 1) * 128]) * scale
            qat_ref[0, h * 128:(h + 1) * 128, :] = jnp.concatenate([q1, q2], axis=0).astype(BF16)
            qna_ref[0, h] = max_sq_norm(q1, q2)

    def put_ak(p):
        for s in range(A_W // LANES):
            blk = jnp.concatenate([rope_a(p[s * 128:s * 128 + 64]),
                                   rope_a(p[s * 128 + 64:(s + 1) * 128])], axis=0)
            ka_ref[0, :, s * LANES:(s + 1) * LANES] = blk.T.astype(BF16)
            kna_ref[0, s] = max_sq_norm(blk)

    def put_v(p, v_ref, heads, dv, vrows):
        ones_rows = (lax.broadcasted_iota(jnp.int32, (vrows - dv, nt.shape[1]), 0) == 0).astype(BF16)
        for h in range(heads):
            v_ref[0, 0, h * vrows:h * vrows + dv, :] = p[h * dv:(h + 1) * dv].astype(BF16)
            v_ref[0, 0, h * vrows + dv:(h + 1) * vrows, :] = ones_rows

    def put_bq(p):
        gq = gq_ref[...]
        for g in range(B_KV_HEADS):
            ys = [rope_b(_rms_rows(p[h * 64:(h + 1) * 64], gq)) * scale
                  for h in range(g * B_REP, (g + 1) * B_REP)]
            qbt_ref[0, g * B_REP * 64:(g + 1) * B_REP * 64, :] = jnp.concatenate(ys, axis=0).astype(BF16)
            qnb_ref[0, g] = max_sq_norm(*ys)

    def put_bk(p):
        gk = gk_ref[...]
        zeros = jnp.zeros((LANES - B_HEAD_DIM, p.shape[1]), F32)
        for g in range(B_KV_HEADS):
            y = rope_b(_rms_rows(p[g * 64:(g + 1) * 64], gk))
            kb_ref[0, g] = jnp.concatenate([y, zeros], axis=0).T.astype(BF16)
            knb_ref[0, g] = max_sq_norm(y)

    off_bq = 3 * A_W
    off_bk = off_bq + B_Q
    p_aq = proj_t(0, A_W)
    p_ak = proj_t(A_W, 2 * A_W)
    put_aq(p_aq)
    p_av = proj_t(2 * A_W, off_bq)
    put_ak(p_ak)
    p_bq = proj_t(off_bq, off_bk)
    put_v(p_av, vat_ref, A_HEADS, 2 * A_HEAD_DIM, A_VROWS)
    p_bkv = proj_t(off_bk, off_bk + 2 * B_KV)
    put_bq(p_bq)
    gates = jnp.dot(nb, wg_ref[...], preferred_element_type=F32)
    put_bk(p_bkv[:B_KV])
    put_v(p_bkv[B_KV:], vbt_ref, B_KV_HEADS, B_HEAD_DIM, B_VROWS)
    ga_ref[0] = jax.nn.sigmoid(gates[:, :D_MODEL])
    gb_ref[0] = jax.nn.sigmoid(gates[:, D_MODEL:])


def _const_spec(shape):
    nd = len(shape)
    return pl.BlockSpec(shape, lambda *_: (0,) * nd, pipeline_mode=pl.Buffered(1))


def _project(x, cos_t, sin_t, g_mix, wt, wg, gq, gk):
    b, s, d = x.shape
    t = min(TOKEN_TILE, s)
    nt = s // t
    out_shape = (
        jax.ShapeDtypeStruct((b, A_W, s), BF16),
        jax.ShapeDtypeStruct((b, s, A_W), BF16),
        jax.ShapeDtypeStruct((b, nt, A_HEADS * A_VROWS, t), BF16),
        jax.ShapeDtypeStruct((b, B_Q, s), BF16),
        jax.ShapeDtypeStruct((b, B_KV_HEADS, s, LANES), BF16),
        jax.ShapeDtypeStruct((b, nt, B_KV_HEADS * B_VROWS, t), BF16),
        jax.ShapeDtypeStruct((b, s, D_MODEL), F32),
        jax.ShapeDtypeStruct((b, s, D_MODEL), F32),
        jax.ShapeDtypeStruct((b, A_HEADS, nt * SUBLANES, LANES), F32),
        jax.ShapeDtypeStruct((b, A_HEADS, nt * SUBLANES, LANES), F32),
        jax.ShapeDtypeStruct((b, B_KV_HEADS, nt * SUBLANES, LANES), F32),
        jax.ShapeDtypeStruct((b, B_KV_HEADS, nt * SUBLANES, LANES), F32),
    )
    in_specs = [
        pl.BlockSpec((1, t, d), lambda bi, i: (bi, i, 0)),
        pl.BlockSpec((40, t), lambda bi, i: (0, i)),
        pl.BlockSpec((40, t), lambda bi, i: (0, i)),
        _const_spec((1, d)),
        _const_spec(wt.shape),
        _const_spec(wg.shape),
        _const_spec((B_HEAD_DIM, 1)),
        _const_spec((B_HEAD_DIM, 1)),
    ]
    out_specs = (
        pl.BlockSpec((1, A_W, t), lambda bi, i: (bi, 0, i)),
        pl.BlockSpec((1, t, A_W), lambda bi, i: (bi, i, 0)),
        pl.BlockSpec((1, 1, A_HEADS * A_VROWS, t), lambda bi, i: (bi, i, 0, 0)),
        pl.BlockSpec((1, B_Q, t), lambda bi, i: (bi, 0, i)),
        pl.BlockSpec((1, B_KV_HEADS, t, LANES), lambda bi, i: (bi, 0, i, 0)),
        pl.BlockSpec((1, 1, B_KV_HEADS * B_VROWS, t), lambda bi, i: (bi, i, 0, 0)),
        pl.BlockSpec((1, t, D_MODEL), lambda bi, i: (bi, i, 0)),
        pl.BlockSpec((1, t, D_MODEL), lambda bi, i: (bi, i, 0)),
        pl.BlockSpec((1, A_HEADS, SUBLANES, LANES), lambda bi, i: (bi, 0, i, 0)),
        pl.BlockSpec((1, A_HEADS, SUBLANES, LANES), lambda bi, i: (bi, 0, i, 0)),
        pl.BlockSpec((1, B_KV_HEADS, SUBLANES, LANES), lambda bi, i: (bi, 0, i, 0)),
        pl.BlockSpec((1, B_KV_HEADS, SUBLANES, LANES), lambda bi, i: (bi, 0, i, 0)),
    )
    return pl.pallas_call(
        _proj_kernel,
        grid=(b, nt),
        in_specs=in_specs,
        out_specs=out_specs,
        out_shape=out_shape,
        compiler_params=pltpu.CompilerParams(
            dimension_semantics=("arbitrary", "arbitrary"), vmem_limit_bytes=VMEM_LIMIT),
        name="proj",
    )(x, cos_t, sin_t, g_mix, wt, wg, gq, gk)


KEY_BLOCK = 256


def _flash_t(q_t, k_rows, v_tile_t, s_refs, acc_ref, n_tiles, track_max):
    s_a, s_b = s_refs
    tk = s_a.shape[0]
    acc_ref[...] = jnp.zeros_like(acc_ref)

    def scores(j, s_ref):
        s = jnp.dot(k_rows(pl.multiple_of(j * tk, tk), tk), q_t, preferred_element_type=F32)
        s_ref[...] = s
        return jnp.max(s, axis=0, keepdims=True) if track_max else None

    def consume(v_t, s_ref, m, bmax):
        if not track_max:
            p = jnp.exp2(s_ref[...]).astype(BF16)
            acc_ref[...] += jnp.dot(v_t, p, preferred_element_type=F32)
            return m
        m_new = jnp.maximum(m, bmax)
        alpha = jnp.exp2(m - m_new)
        p = jnp.exp2(s_ref[...] - m_new).astype(BF16)
        acc_ref[...] = alpha * acc_ref[...] + jnp.dot(v_t, p, preferred_element_type=F32)
        return m_new

    def tile(i, carry, last):
        m, bmax_a = carry
        v = v_tile_t(i)
        assert v.shape[1] == 2 * tk
        bmax_b = scores(2 * i + 1, s_b)
        m = consume(v[:, :tk], s_a, m, bmax_a)
        if not last:
            bmax_a = scores(2 * i + 2, s_a)
        m = consume(v[:, tk:], s_b, m, bmax_b)
        return m, bmax_a

    bmax = scores(0, s_a)
    carry = (jnp.full((1, q_t.shape[1]), NEG_BIG, F32), bmax) if track_max else (None, None)
    n_loop = n_tiles - 1
    unroll = next(u for u in ((5, 3, 1) if track_max else (15, 5, 3, 1)) if n_loop % u == 0)
    carry = lax.fori_loop(0, n_loop, lambda i, c: tile(i, c, False), carry, unroll=unroll)
    tile(n_tiles - 1, carry, True)


SCORE_BOUND = 64.0
BOUND_SLACK = 1.05


def _both_paths(qn_ref, kn_ref, q_tiles, n_tiles, body):
    b, h = pl.program_id(0), pl.program_id(1)
    q_sq = functools.reduce(jnp.maximum, [qn_ref[b, h, t] for t in q_tiles])
    k_sq = functools.reduce(jnp.maximum, [kn_ref[b, h, t] for t in range(n_tiles)])
    bounded = q_sq * k_sq * (BOUND_SLACK * BOUND_SLACK) <= SCORE_BOUND * SCORE_BOUND
    pl.when(bounded)(functools.partial(body, False))
    pl.when(jnp.logical_not(bounded))(functools.partial(body, True))


Q_SUBTILES = 2


def _attn_a_kernel(qn_ref, kn_ref, lq1_ref, lk1_ref, lq2_ref, lk2_ref, gd_ref, qt_ref, k_ref, vt_ref,
                   o_ref, s_a, s_b, *acc_refs):
    tq = qt_ref.shape[2] // Q_SUBTILES
    n_tiles, tile = vt_ref.shape[1], vt_ref.shape[3]
    first = pl.program_id(2) * Q_SUBTILES

    def body(track_max):
        for sub, acc_ref in enumerate(acc_refs):
            q = qt_ref[0, :, sub * tq:(sub + 1) * tq]
            row = lax.broadcasted_iota(jnp.int32, q.shape, 0)
            zero = jnp.zeros_like(q)
            q_t = jnp.concatenate([jnp.where(row < A_HEAD_DIM, q, zero),
                                   jnp.where(row >= A_HEAD_DIM, q, zero)], axis=1)
            _flash_t(q_t, lambda r0, n: k_ref[0, pl.ds(r0, n), :], lambda i: vt_ref[0, i],
                     (s_a, s_b), acc_ref, n_tiles, track_max)

            lam = (jnp.exp(jnp.sum(lq1_ref[...] * lk1_ref[...], axis=-1, keepdims=True))
                   - jnp.exp(jnp.sum(lq2_ref[...] * lk2_ref[...], axis=-1, keepdims=True))
                   + LAM_INIT)
            dv = 2 * A_HEAD_DIM
            acc = acc_ref[0:dv, :]
            l = acc_ref[dv:dv + 1, :]
            o = acc[:, :tq] / l[:, :tq] - lam * (acc[:, tq:] / l[:, tq:])
            y = _rms_rows(o, gd_ref[...]) * (1.0 - LAM_INIT)
            o_ref[0, sub * tq:(sub + 1) * tq, :] = y.T.astype(BF16)

    _both_paths(qn_ref, kn_ref, [(first + sub) * tq // tile for sub in range(Q_SUBTILES)], n_tiles, body)


def _attn_b_kernel(qn_ref, kn_ref, qt_ref, k_ref, vt_ref, o_ref, s_a, s_b, *acc_refs):
    tq = qt_ref.shape[2] // Q_SUBTILES
    n_tiles, tile = vt_ref.shape[1], vt_ref.shape[3]
    first = pl.program_id(2) * Q_SUBTILES

    def body(track_max):
        for sub, acc_ref in enumerate(acc_refs):
            q = qt_ref[0, :, sub * tq:(sub + 1) * tq]
            q_cat = jnp.concatenate([q[r * 64:(r + 1) * 64] for r in range(B_REP)], axis=1)
            q_t = jnp.concatenate([q_cat, jnp.zeros_like(q_cat)], axis=0)
            _flash_t(q_t, lambda r0, n: k_ref[0, 0, pl.ds(r0, n), :], lambda i: vt_ref[0, i],
                     (s_a, s_b), acc_ref, n_tiles, track_max)

            o = acc_ref[0:B_HEAD_DIM, :] / acc_ref[B_HEAD_DIM:B_HEAD_DIM + 1, :]
            o_rows = jnp.concatenate([o[:, r * tq:(r + 1) * tq] for r in range(B_REP)], axis=0)
            o_ref[0, sub * tq:(sub + 1) * tq, :] = o_rows.T.astype(BF16)

    _both_paths(qn_ref, kn_ref, [(first + sub) * tq // tile for sub in range(Q_SUBTILES)], n_tiles, body)


def _tile_scalars(stat):
    return stat[:, :, ::SUBLANES, 0]


def _attention_a(lq1, lk1, lq2, lk2, g_diff, qat, ka, vat, qn, kn):
    b, _, s = qat.shape
    tk = vat.shape[3]
    tq = A_Q_TILE
    tstep = Q_SUBTILES * tq
    small = lambda shape: pl.BlockSpec(shape, lambda bi, h, i, *_: (0,) * len(shape))
    return pl.pallas_call(
        _attn_a_kernel,
        grid_spec=pltpu.PrefetchScalarGridSpec(
            num_scalar_prefetch=2,
            grid=(b, A_HEADS, s // tstep),
            in_specs=[
                small((1, A_HEAD_DIM)), small((1, A_HEAD_DIM)), small((1, A_HEAD_DIM)), small((1, A_HEAD_DIM)),
                small((2 * A_HEAD_DIM, 1)),
                pl.BlockSpec((1, 2 * A_HEAD_DIM, tstep), lambda bi, h, i, *_: (bi, h, i)),
                pl.BlockSpec((1, s, 2 * A_HEAD_DIM), lambda bi, h, i, *_: (bi, 0, h)),
                pl.BlockSpec((1, s // tk, A_VROWS, tk), lambda bi, h, i, *_: (bi, 0, h, 0)),
            ],
            out_specs=pl.BlockSpec((1, tstep, 2 * A_HEAD_DIM), lambda bi, h, i, *_: (bi, i, h)),
            scratch_shapes=([pltpu.VMEM((KEY_BLOCK, 2 * tq), F32)] * 2
                            + [pltpu.VMEM((A_VROWS, 2 * tq), F32)] * Q_SUBTILES)),
        out_shape=jax.ShapeDtypeStruct((b, s, A_W), BF16),
        compiler_params=pltpu.CompilerParams(
            dimension_semantics=("parallel", "parallel", "arbitrary"), vmem_limit_bytes=VMEM_LIMIT),
        name="attn_a",
    )(_tile_scalars(qn), _tile_scalars(kn), lq1, lk1, lq2, lk2, g_diff, qat, ka, vat)


def _attention_b(qbt, kb, vbt, qn, kn):
    b, _, s = qbt.shape
    tk = vbt.shape[3]
    tq = B_Q_TILE
    tstep = Q_SUBTILES * tq
    rows = B_REP * B_HEAD_DIM
    return pl.pallas_call(
        _attn_b_kernel,
        grid_spec=pltpu.PrefetchScalarGridSpec(
            num_scalar_prefetch=2,
            grid=(b, B_KV_HEADS, s // tstep),
            in_specs=[
                pl.BlockSpec((1, rows, tstep), lambda bi, g, i, *_: (bi, g, i)),
                pl.BlockSpec((1, 1, s, LANES), lambda bi, g, i, *_: (bi, g, 0, 0)),
                pl.BlockSpec((1, s // tk, B_VROWS, tk), lambda bi, g, i, *_: (bi, 0, g, 0)),
            ],
            out_specs=pl.BlockSpec((1, tstep, rows), lambda bi, g, i, *_: (bi, i, g)),
            scratch_shapes=([pltpu.VMEM((KEY_BLOCK, B_REP * tq), F32)] * 2
                            + [pltpu.VMEM((B_VROWS, B_REP * tq), F32)] * Q_SUBTILES)),
        out_shape=jax.ShapeDtypeStruct((b, s, B_Q), BF16),
        compiler_params=pltpu.CompilerParams(
            dimension_semantics=("parallel", "parallel", "arbitrary"), vmem_limit_bytes=VMEM_LIMIT),
        name="attn_b",
    )(_tile_scalars(qn), _tile_scalars(kn), qbt, kb, vbt)


def _merge_kernel(x_ref, oa_ref, ob_ref, ga_ref, gb_ref, wa_ref, wb_ref, wo_ref, h_ref):
    ya = jnp.dot(oa_ref[0], wa_ref[...], preferred_element_type=F32)
    yb = jnp.dot(ob_ref[0], wb_ref[...], preferred_element_type=F32)
    merged = ga_ref[0] * ya + gb_ref[0] * yb
    h_ref[0] = x_ref[0] + jnp.dot(merged.astype(BF16), wo_ref[...], preferred_element_type=F32)


def _merge(x, oa, ob, ga, gb, wa, wb, wo):
    b, s, d = x.shape
    t = min(MERGE_TILE, s)
    tile = pl.BlockSpec((1, t, d), lambda bi, i: (bi, i, 0))
    return pl.pallas_call(
        _merge_kernel,
        grid=(b, s // t),
        in_specs=[tile, tile, tile, tile, tile,
                  _const_spec(wa.shape), _const_spec(wb.shape), _const_spec(wo.shape)],
        out_specs=tile,
        out_shape=jax.ShapeDtypeStruct((b, s, d), F32),
        compiler_params=pltpu.CompilerParams(
            dimension_semantics=("arbitrary", "arbitrary"), vmem_limit_bytes=VMEM_LIMIT),
        name="merge",
    )(x, oa, ob, ga, gb, wa, wb, wo)


def _ffn_kernel(hp_ref, h_ref, hn_ref, p_ref, gffn_ref, wup_ref, cw_ref, cb_ref, wdn_ref,
                wple_ref, gple_ref, wpg_ref, gfin_ref, y_ref, ug_ref, uv_ref, a_ref):
    h = h_ref[0]
    t = h.shape[0]
    rows = t + 2 * HALO
    first_row = pl.program_id(1) * t
    seq_len = pl.num_programs(1) * t
    hh = jnp.concatenate([hp_ref[0], h, hn_ref[0]], axis=0)
    r = lax.broadcasted_iota(jnp.int32, (rows, 1), 0) + (first_row - HALO)
    inside = jnp.logical_and(r >= 0, r < seq_len)
    n2 = jnp.where(inside, _rms(hh, gffn_ref[...]), 0.0).astype(BF16)

    def up_proj(c):
        c0 = c * FFN_CHUNK
        ug_ref[c % 2] = jnp.dot(n2, wup_ref[:, c0:c0 + FFN_CHUNK], preferred_element_type=F32)
        uv_ref[c % 2] = jnp.dot(n2, wup_ref[:, D_FF + c0:D_FF + c0 + FFN_CHUNK],
                                preferred_element_type=F32)

    def conv(u_ref, slot, c0):
        w = cw_ref[:, c0:c0 + FFN_CHUNK]
        um = u_ref[slot, HALO - 1:HALO - 1 + t, :]
        uc = u_ref[slot, HALO:HALO + t, :]
        up = u_ref[slot, HALO + 1:HALO + 1 + t, :]
        return um * w[0:1] + uc * w[1:2] + up * w[2:3] + cb_ref[:, c0:c0 + FFN_CHUNK]

    n_chunks = D_FF // FFN_CHUNK
    split = (n_chunks + 1) // 2 * FFN_CHUNK
    up_proj(0)
    for c in range(n_chunks):
        if c + 1 < n_chunks:
            up_proj(c + 1)
        c0 = c * FFN_CHUNK
        a = jax.nn.gelu(conv(ug_ref, c % 2, c0)) * conv(uv_ref, c % 2, D_FF + c0)
        a_ref[:, c0:c0 + FFN_CHUNK] = a.astype(BF16)
        if c0 + FFN_CHUNK == split:
            down = jnp.dot(a_ref[:, :split], wdn_ref[:split, :], preferred_element_type=F32)
    down = down + jnp.dot(a_ref[:, split:], wdn_ref[split:, :], preferred_element_type=F32)
    h2 = h + down

    n3 = _rms(h2, gple_ref[...]).astype(BF16)
    gate = jax.nn.sigmoid(jnp.dot(n3, wpg_ref[...], preferred_element_type=F32))
    e = jnp.dot(p_ref[0].astype(BF16), wple_ref[...], preferred_element_type=F32)
    h3 = h2 + e * gate
    y_ref[0] = _rms(h3, gfin_ref[...])


def _ffn(h, p, g_ffn, wup, conv_w, conv_b, wdn, wple, g_ple, wpg, g_final):
    b, s, d = h.shape
    t = min(FFN_TILE, s)
    per = t // HALO
    nhb = s // HALO
    tile = pl.BlockSpec((1, t, d), lambda bi, i: (bi, i, 0))
    return pl.pallas_call(
        _ffn_kernel,
        grid=(b, s // t),
        in_specs=[
            pl.BlockSpec((1, HALO, d), lambda bi, i: (bi, jnp.maximum(i * per - 1, 0), 0)),
            tile,
            pl.BlockSpec((1, HALO, d), lambda bi, i: (bi, jnp.minimum((i + 1) * per, nhb - 1), 0)),
            pl.BlockSpec((1, t, PLE_DIM), lambda bi, i: (bi, i, 0)),
            _const_spec((1, d)),
            _const_spec(wup.shape),
            _const_spec(conv_w.shape),
            _const_spec(conv_b.shape),
            _const_spec(wdn.shape),
            _const_spec(wple.shape),
            _const_spec((1, d)),
            _const_spec(wpg.shape),
            _const_spec((1, d)),
        ],
        out_specs=tile,
        out_shape=jax.ShapeDtypeStruct((b, s, d), F32),
        scratch_shapes=[pltpu.VMEM((2, t + 2 * HALO, FFN_CHUNK), F32),
                        pltpu.VMEM((2, t + 2 * HALO, FFN_CHUNK), F32),
                        pltpu.VMEM((t, D_FF), BF16)],
        compiler_params=pltpu.CompilerParams(
            dimension_semantics=("arbitrary", "arbitrary"), vmem_limit_bytes=VMEM_LIMIT),
        name="ffn",
    )(h, h, h, p, g_ffn, wup, conv_w, conv_b, wdn, wple, g_ple, wpg, g_final)


def _rope_tables_t(s):
    pos = jnp.arange(s, dtype=jnp.int32)

    def tab(ids, dim, theta):
        inv = theta ** (-jnp.arange(0, dim, 2, dtype=F32) / dim)
        ang = ids.astype(F32)[:, None] * inv[None, :]
        return jnp.cos(ang).T, jnp.sin(ang).T

    ca, sa = tab(pos, A_ROT_DIM, A_ROPE_THETA)
    cr, sr = tab(pos // GRID_W, B_HEAD_DIM // 2, B_AXIAL_THETA)
    cc, sc = tab(pos % GRID_W, B_HEAD_DIM // 2, B_AXIAL_THETA)
    return jnp.concatenate([ca, cr, cc], axis=0), jnp.concatenate([sa, sr, sc], axis=0)


def _encode(x, p, cos_t, sin_t, w):
    qat, ka, vat, qbt, kb, vbt, ga, gb, qna, kna, qnb, knb = _project(
        x, cos_t, sin_t, w["g_mix"], w["wt"], w["wg"], w["g_qn"], w["g_kn"])
    oa = _attention_a(w["lq1"], w["lk1"], w["lq2"], w["lk2"], w["g_diff"], qat, ka, vat, qna, kna)
    ob = _attention_b(qbt, kb, vbt, qnb, knb)
    h1 = _merge(x, oa, ob, ga, gb, w["w_a"], w["w_b"], w["w_out"])
    return _ffn(h1, p, w["g_ffn"], w["w_up"], w["conv_w"], w["conv_b"], w["w_down"],
                w["w_ple"], w["g_ple"], w["w_ple_gate"], w["g_final"])


def kernel(x_prompt, x_sample, p_prompt, p_sample, g_mix, w_in, lambda_q1, lambda_k1, lambda_q2,
           lambda_k2, g_diff, w_a, g_qn, g_kn, w_b, w_out, g_ffn, w_up, conv_w, conv_b, w_down,
           w_ple, g_ple, w_ple_gate, g_final):
    w_in0 = w_in[0]
    row = lambda v: v.reshape(1, -1).astype(F32)
    col = lambda v: v.reshape(-1, 1).astype(F32)
    w = dict(
        g_mix=row(g_mix[0]),
        wt=w_in0[:, :N_TRANSPOSED].T.astype(BF16),
        wg=w_in0[:, N_TRANSPOSED:].astype(BF16),
        g_qn=col(g_qn[0]), g_kn=col(g_kn[0]),
        lq1=row(lambda_q1[0]), lk1=row(lambda_k1[0]), lq2=row(lambda_q2[0]), lk2=row(lambda_k2[0]),
        g_diff=col(g_diff[0]),
        w_a=w_a[0].astype(BF16), w_b=w_b[0].astype(BF16), w_out=w_out[0].astype(BF16),
        g_ffn=row(g_ffn[0]),
        w_up=w_up[0].astype(BF16), conv_w=conv_w[0].astype(F32), conv_b=row(conv_b[0]),
        w_down=w_down[0].astype(BF16),
        w_ple=w_ple[0].astype(BF16), g_ple=row(g_ple[0]), w_ple_gate=w_ple_gate[0].astype(BF16),
        g_final=row(g_final),
    )
    outs = []
    for x, p in ((x_prompt, p_prompt), (x_sample, p_sample)):
        cos_t, sin_t = _rope_tables_t(x.shape[1])
        outs.append(_encode(x, p[0], cos_t, sin_t, w))
    return tuple(outs)
```
